```python
import jax, jax.numpy as jnp
from jax import lax
import numpy as np

D_MODEL = 4096
BATCH = 2
SEQ = 4096
DEPTH = 2

GRID_W = 64
CTX_LEN = 256
MIX_W = D_MODEL
M_WIDTH = MIX_W // 2
M_HEADS = 4
M_DV = M_WIDTH // M_HEADS
M_DK = M_DV // 2
CHUNK = 128
A_WIDTH = MIX_W - M_WIDTH
HEAD_DIM = 128
A_HEADS = A_WIDTH // HEAD_DIM
KV_HEADS = A_HEADS // 4
Q_BLOCK = 128
ROPE_AXIS_DIM = HEAD_DIM // 2
ROPE_THETA = 10000.0
EPS = 1e-6
FORGET_BIAS = 3.0
SPLITS = (M_HEADS * M_DK, M_HEADS * M_DK, M_WIDTH, M_WIDTH, M_WIDTH, 4 * M_HEADS,
          A_WIDTH, KV_HEADS * HEAD_DIM, KV_HEADS * HEAD_DIM, A_WIDTH)
IN_COLS = sum(SPLITS)

kernel_name = "hymba_mlstm_gqa_prefix_dit"


def rmsnorm(x, g):
    x32 = x.astype(jnp.float32)
    y = x32 * lax.rsqrt(jnp.mean(x32 * x32, axis=-1, keepdims=True) + EPS)
    return (y * g.astype(jnp.float32)).astype(x.dtype)


def split_cols(p):
    idx, acc = [], 0
    for s in SPLITS[:-1]:
        acc += s
        idx.append(acc)
    return jnp.split(p, idx, axis=-1)


def rope_tables(T):
    rows = T // GRID_W
    row_ids = jnp.repeat(jnp.arange(rows), GRID_W).astype(jnp.float32)
    col_ids = jnp.tile(jnp.arange(GRID_W), rows).astype(jnp.float32)
    inv = ROPE_THETA ** (-jnp.arange(0, ROPE_AXIS_DIM, 2, dtype=jnp.float32) / ROPE_AXIS_DIM)
    ang = jnp.concatenate([row_ids[:, None] * inv, col_ids[:, None] * inv], axis=-1)
    return jnp.cos(ang), jnp.sin(ang)


def apply_rope_2d(x, cos, sin):
    xf = x.astype(jnp.float32)
    nf = ROPE_AXIS_DIM // 2

    def rot(xp, cs, sn):
        x1, x2 = xp[..., :nf], xp[..., nf:]
        cs, sn = cs[None, :, None, :], sn[None, :, None, :]
        return jnp.concatenate([x1 * cs - x2 * sn, x1 * sn + x2 * cs], axis=-1)

    out = jnp.concatenate([
        rot(xf[..., :ROPE_AXIS_DIM], cos[:, :nf], sin[:, :nf]),
        rot(xf[..., ROPE_AXIS_DIM:], cos[:, nf:], sin[:, nf:])], axis=-1)
    return out.astype(x.dtype)


def mlstm_chunked(q, k, v, log_i, log_f, state):
    B, H, T, _ = q.shape
    DV = v.shape[-1]
    nc = T // CHUNK

    def chunks(a):
        return jnp.moveaxis(a.reshape((B, H, nc, CHUNK) + a.shape[3:]), 2, 0)

    tril = jnp.tril(jnp.ones((CHUNK, CHUNK), dtype=bool))

    def step(carry, inp):
        C, n, m = carry
        qc, kc, vc, ic, fc = inp
        b = jnp.cumsum(fc, axis=-1)
        logw = b[..., :, None] - b[..., None, :] + ic[..., None, :]
        logw = jnp.where(tril, logw, -jnp.inf)
        m_state = b + m[..., None]
        m_t = jnp.maximum(m_state, jnp.max(logw, axis=-1))
        w_state = jnp.exp(m_state - m_t)
        s = jnp.einsum('bhtd,bhsd->bhts', qc, kc) * jnp.exp(logw - m_t[..., None])
        num = (w_state[..., None] * jnp.einsum('bhvd,bhtd->bhtv', C, qc)
               + jnp.einsum('bhts,bhsv->bhtv', s, vc))
        den = w_state * jnp.einsum('bhd,bhtd->bht', n, qc) + jnp.sum(s, axis=-1)
        h = num / jnp.maximum(jnp.abs(den), jnp.exp(-m_t))[..., None]
        m_new = m_t[..., -1]
        w_s = jnp.exp(b[..., -1:] - b + ic - m_new[..., None])
        a_state = jnp.exp(b[..., -1] + m - m_new)
        C_new = a_state[..., None, None] * C + jnp.einsum('bhs,bhsv,bhsd->bhvd', w_s, vc, kc)
        n_new = a_state[..., None] * n + jnp.einsum('bhs,bhsd->bhd', w_s, kc)
        return (C_new, n_new, m_new), h

    state, hs = lax.scan(step, state, tuple(chunks(a) for a in (q, k, v, log_i, log_f)))
    return jnp.moveaxis(hs, 0, 2).reshape(B, H, T, DV), state


def mlstm_branch(parts_ctx, parts_lat, b_gate, g_mlstm):
    def prep(parts):
        q, k, v, o, z, g = parts
        B, T, _ = q.shape

        def heads(a, d):
            return a.reshape(B, T, M_HEADS, d).transpose(0, 2, 1, 3).astype(jnp.float32)

        gates = (g.astype(jnp.float32) + b_gate.astype(jnp.float32)).reshape(B, T, 4, M_HEADS)
        gates = gates.transpose(2, 0, 3, 1)
        return dict(q=heads(q, M_DK) * (M_DK ** -0.5), k=heads(k, M_DK), v=heads(v, M_DV),
                    o=o.astype(jnp.float32), z=z.astype(jnp.float32),
                    i_f=gates[0], f_f=jax.nn.log_sigmoid(gates[1]),
                    i_b=gates[2], f_b=jax.nn.log_sigmoid(gates[3]))

    pc, pl = prep(parts_ctx), prep(parts_lat)
    B = pl['q'].shape[0]
    zero = (jnp.zeros((B, M_HEADS, M_DV, M_DK), jnp.float32),
            jnp.zeros((B, M_HEADS, M_DK), jnp.float32),
            jnp.zeros((B, M_HEADS), jnp.float32))
    flip = lambda a: jnp.flip(a, axis=2)

    h_cf, st_cf = mlstm_chunked(pc['q'], pc['k'], pc['v'], pc['i_f'], pc['f_f'], zero)
    h_lf, _ = mlstm_chunked(pl['q'], pl['k'], pl['v'], pl['i_f'], pl['f_f'], st_cf)
    h_cb, st_cb = mlstm_chunked(flip(pc['q']), flip(pc['k']), flip(pc['v']),
                                flip(pc['i_b']), flip(pc['f_b']), zero)
    h_lb, _ = mlstm_chunked(flip(pl['q']), flip(pl['k']), flip(pl['v']),
                            flip(pl['i_b']), flip(pl['f_b']), st_cb)
    gm = g_mlstm.astype(jnp.float32).reshape(M_HEADS, 1, M_DV)

    def finish(h, p, dtype):
        hn = h * lax.rsqrt(jnp.mean(h * h, axis=-1, keepdims=True) + EPS) * gm
        B_, _, T_, _ = hn.shape
        hn = hn.transpose(0, 2, 1, 3).reshape(B_, T_, M_WIDTH)
        return (hn * jax.nn.sigmoid(p['o']) * jax.nn.silu(p['z'])).astype(dtype)

    out_lat = finish(h_lf + flip(h_lb), pl, parts_lat[0].dtype)
    out_ctx = finish(h_cf + flip(h_cb), pc, parts_ctx[0].dtype)
    return out_lat, out_ctx


def gqa_block(qb, k_all, v_all):
    s = jnp.einsum('bqhgd,bkhd->bhgqk', qb, k_all).astype(jnp.float32) * (HEAD_DIM ** -0.5)
    p = jax.nn.softmax(s, axis=-1)
    return jnp.einsum('bhgqk,bkhd->bqhgd', p.astype(v_all.dtype), v_all)


def attn_branch(parts_ctx, parts_lat, g_q, g_k, rope_cos, rope_sin, update_ctx):
    G = A_HEADS // KV_HEADS

    def prep(parts):
        q, k, v, z = parts
        B, T, _ = q.shape
        q = rmsnorm(q.reshape(B, T, A_HEADS, HEAD_DIM), g_q)
        k = rmsnorm(k.reshape(B, T, KV_HEADS, HEAD_DIM), g_k)
        return q, k, v.reshape(B, T, KV_HEADS, HEAD_DIM), z

    qc, kc, vc, zc = prep(parts_ctx)
    ql, kl, vl, zl = prep(parts_lat)
    ql = apply_rope_2d(ql, rope_cos, rope_sin)
    kl = apply_rope_2d(kl, rope_cos, rope_sin)
    k_all = jnp.concatenate([kc, kl], axis=1)
    v_all = jnp.concatenate([vc, vl], axis=1)
    B, T = ql.shape[0], ql.shape[1]
    nb = T // Q_BLOCK
    qs = ql.reshape(B, nb, Q_BLOCK, KV_HEADS, G, HEAD_DIM).transpose(1, 0, 2, 3, 4, 5)
    outs = lax.map(lambda qb: gqa_block(qb, k_all, v_all), qs)
    o_lat = outs.transpose(1, 0, 2, 3, 4, 5).reshape(B, T, A_WIDTH)
    out_lat = o_lat * jax.nn.silu(zl)
    out_ctx = None
    if update_ctx:
        Tc = qc.shape[1]
        o_ctx = gqa_block(qc.reshape(B, Tc, KV_HEADS, G, HEAD_DIM), kc, vc).reshape(B, Tc, A_WIDTH)
        out_ctx = o_ctx * jax.nn.silu(zc)
    return out_lat, out_ctx


def hybrid_layer(x, ctx, c, c_ctx, w_mod, b_mod, g_norm, w_in, b_gate, g_mlstm, g_q, g_k,
                 w_out, rope_cos, rope_sin, update_ctx):
    shift, scale, gate = jnp.split(jax.nn.silu(c) @ w_mod + b_mod, 3, axis=-1)
    shift_c, scale_c, gate_c = jnp.split(jax.nn.silu(c_ctx) @ w_mod + b_mod, 3, axis=-1)
    h = rmsnorm(x, g_norm) * (1 + scale[:, None, :]) + shift[:, None, :]
    hc = rmsnorm(ctx, g_norm) * (1 + scale_c) + shift_c
    parts = split_cols(h @ w_in)
    parts_c = split_cols(hc @ w_in)
    m_lat, m_ctx = mlstm_branch(parts_c[:6], parts[:6], b_gate, g_mlstm)
    a_lat, a_ctx = attn_branch(parts_c[6:], parts[6:], g_q, g_k, rope_cos, rope_sin, update_ctx)
    x = x + gate[:, None, :] * (jnp.concatenate([m_lat, a_lat], axis=-1) @ w_out)
    if update_ctx:
        ctx = ctx + gate_c * (jnp.concatenate([m_ctx, a_ctx], axis=-1) @ w_out)
    return x, ctx


def setup_inputs(seed: int = 0) -> dict:
    key = jax.random.key(seed)
    ks = jax.random.split(key, 16)
    nrm = jax.random.normal
    f32 = jnp.float32
    b_gate = (0.1 * nrm(ks[8], (DEPTH, 4, M_HEADS), f32)
              + jnp.array([0.0, FORGET_BIAS, 0.0, FORGET_BIAS], f32)[None, :, None])
    return {
        "x": nrm(ks[0], (BATCH, SEQ, D_MODEL), f32),
        "c": nrm(ks[1], (BATCH, D_MODEL), f32),
        "ctx": nrm(ks[2], (BATCH, CTX_LEN, D_MODEL), f32),
        "c_ctx": nrm(ks[3], (D_MODEL,), f32),
        "w_mod": nrm(ks[4], (DEPTH, D_MODEL, 3 * D_MODEL), f32) * (0.5 * D_MODEL ** -0.5),
        "b_mod": 0.01 * nrm(ks[5], (DEPTH, 3 * D_MODEL), f32),
        "g_norm": 1.0 + 0.05 * nrm(ks[6], (DEPTH, D_MODEL), f32),
        "w_in": nrm(ks[7], (DEPTH, D_MODEL, IN_COLS), f32) * (D_MODEL ** -0.5),
        "b_gate": b_gate.reshape(DEPTH, 4 * M_HEADS),
        "g_mlstm": 1.0 + 0.05 * nrm(ks[9], (DEPTH, M_WIDTH), f32),
        "g_q": 1.0 + 0.05 * nrm(ks[10], (DEPTH, HEAD_DIM), f32),
        "g_k": 1.0 + 0.05 * nrm(ks[11], (DEPTH, HEAD_DIM), f32),
        "w_out": nrm(ks[12], (DEPTH, MIX_W, D_MODEL), f32) * (MIX_W ** -0.5),
        "g_final": 1.0 + 0.05 * nrm(ks[13], (D_MODEL,), f32),
    }


def reference(x, c, ctx, c_ctx, w_mod, b_mod, g_norm, w_in, b_gate, g_mlstm, g_q, g_k,
              w_out, g_final):
    rope_cos, rope_sin = rope_tables(x.shape[1])
    for layer in range(DEPTH):
        x, ctx = hybrid_layer(x, ctx, c, c_ctx, w_mod[layer], b_mod[layer], g_norm[layer],
                              w_in[layer], b_gate[layer], g_mlstm[layer], g_q[layer], g_k[layer],
                              w_out[layer], rope_cos, rope_sin, layer < DEPTH - 1)
    return rmsnorm(x, g_final)
```

```python
import functools
import math

import jax
import jax.numpy as jnp
from jax import lax
from jax.experimental import pallas as pl
from jax.experimental.pallas import tpu as pltpu

CHUNK = 128
M_HEADS = 4
HEAD_DIM = 128
GQA_GROUP = 4
GRID_W = 64
ROPE_THETA = 10000.0
EPS = 1e-6
N_GATES = 4 * M_HEADS

LANES = 128
V7X_VMEM_LIMIT_BYTES = 56 * 1024 * 1024

F32 = jnp.float32
BF16 = jnp.bfloat16


def _params(*sem):
    return pltpu.CompilerParams(dimension_semantics=sem, vmem_limit_bytes=V7X_VMEM_LIMIT_BYTES)


def _tile(total, target, multiple):
    best = None
    for t in range(multiple, min(total, target) + 1, multiple):
        if total % t == 0:
            best = t
    assert best is not None, (total, target, multiple)
    return best


def _dot(a, b):
    return jnp.dot(a, b, preferred_element_type=F32)


def _dot_nt(a, b):
    return lax.dot_general(a, b, (((1,), (1,)), ((), ())), preferred_element_type=F32)


def _dot_tn(a, b):
    return lax.dot_general(a, b, (((0,), (0,)), ((), ())), preferred_element_type=F32)


def _mod_kernel(c_ref, w_ref, b_ref, o_ref):
    c = c_ref[...]
    a = (c * jax.nn.sigmoid(c)).astype(BF16)
    o_ref[...] = _dot(a, w_ref[...].astype(BF16)) + b_ref[...]


def _modulation(c_rows, w_mod, b_mod):
    depth, d, n = w_mod.shape
    rows = c_rows.shape[0]
    tn = _tile(n, 512, LANES)
    return pl.pallas_call(
        _mod_kernel,
        grid=(depth, n // tn),
        in_specs=[
            pl.BlockSpec((rows, d), lambda l, j: (0, 0)),
            pl.BlockSpec((None, d, tn), lambda l, j: (l, 0, j)),
            pl.BlockSpec((None, 1, tn), lambda l, j: (l, 0, j)),
        ],
        out_specs=pl.BlockSpec((None, rows, tn), lambda l, j: (l, 0, j)),
        out_shape=jax.ShapeDtypeStruct((depth, rows, n), F32),
        compiler_params=_params("arbitrary", "arbitrary"),
        name="modulation",
    )(c_rows, w_mod, b_mod.reshape(depth, 1, n))


def _prologue_kernel(x_ref, g_ref, scale_ref, shift_ref, wg_ref, bg_ref, h_ref, gact_ref):
    x = x_ref[...]
    y = x * lax.rsqrt(jnp.mean(x * x, axis=-1, keepdims=True) + EPS)
    h = (y * g_ref[...]) * (1.0 + scale_ref[...]) + shift_ref[...]
    hb = h.astype(BF16)
    h_ref[...] = hb
    pre = _dot(hb, wg_ref[...]) + bg_ref[...]
    col = lax.broadcasted_iota(jnp.int32, pre.shape, 1)
    is_forget = (col & M_HEADS) != 0
    log_sig = jnp.minimum(pre, 0.0) - jnp.log1p(jnp.exp(-jnp.abs(pre)))
    gact_ref[...] = jnp.where(is_forget, log_sig, pre)


def _prologue(x_all, mod_rows, g_norm, w_gate, b_gate, ctx_len):
    bsz, t_all, d = x_all.shape
    tm = _tile(math.gcd(ctx_len, t_all - ctx_len), 256, 16)
    n_ctx_tiles = ctx_len // tm

    def mod_row(b, t):
        return jnp.where(t < n_ctx_tiles, bsz, b)

    return pl.pallas_call(
        _prologue_kernel,
        grid=(bsz, t_all // tm),
        in_specs=[
            pl.BlockSpec((None, tm, d), lambda b, t: (b, t, 0)),
            pl.BlockSpec((1, d), lambda b, t: (0, 0)),
            pl.BlockSpec((None, None, 1, d), lambda b, t: (mod_row(b, t), 1, 0, 0)),
            pl.BlockSpec((None, None, 1, d), lambda b, t: (mod_row(b, t), 0, 0, 0)),
            pl.BlockSpec((d, LANES), lambda b, t: (0, 0)),
            pl.BlockSpec((1, LANES), lambda b, t: (0, 0)),
        ],
        out_specs=[
            pl.BlockSpec((None, tm, d), lambda b, t: (b, t, 0)),
            pl.BlockSpec((None, tm, LANES), lambda b, t: (b, t, 0)),
        ],
        out_shape=[
            jax.ShapeDtypeStruct((bsz, t_all, d), BF16),
            jax.ShapeDtypeStruct((bsz, t_all, LANES), F32),
        ],
        compiler_params=_params("arbitrary", "arbitrary"),
        name="prologue",
    )(x_all, g_norm.reshape(1, d), mod_rows, mod_rows, w_gate, b_gate)


def _matmul_kernel(a_ref, w_ref, o_ref):
    o_ref[...] = _dot(a_ref[...], w_ref[...])


def _in_projection(h2d, w):
    m, k = h2d.shape
    n = w.shape[1]
    tm = _tile(m, 1152, 16)
    tn = _tile(n, 1024, LANES)
    return pl.pallas_call(
        _matmul_kernel,
        grid=(m // tm, n // tn),
        in_specs=[
            pl.BlockSpec((tm, k), lambda i, j: (i, 0)),
            pl.BlockSpec((k, tn), lambda i, j: (0, j)),
        ],
        out_specs=pl.BlockSpec((tm, tn), lambda i, j: (i, j)),
        out_shape=jax.ShapeDtypeStruct((m, n), F32),
        compiler_params=_params("arbitrary", "arbitrary"),
        name="in_projection",
    )(h2d, w)


def _mlstm_chunk(q, k, v, i_col, f_col, i_row, f_row, ct, n, m, reverse):
    L = q.shape[0]
    t_idx = lax.broadcasted_iota(jnp.int32, (L, L), 0)
    s_idx = lax.broadcasted_iota(jnp.int32, (L, L), 1)
    if reverse:
        seen = s_idx >= t_idx
        seen_t = t_idx >= s_idx
        last = 0
    else:
        seen = s_idx <= t_idx
        seen_t = t_idx <= s_idx
        last = L - 1
    b_col = jnp.sum(jnp.where(seen, f_row, 0.0), axis=1, keepdims=True)
    b_row = jnp.sum(jnp.where(seen_t, f_col, 0.0), axis=0, keepdims=True)
    logw = jnp.where(seen, b_col - b_row + i_row, -jnp.inf)
    m_state = b_col + m
    m_t = jnp.maximum(m_state, jnp.max(logw, axis=1, keepdims=True))
    w_state = jnp.exp(m_state - m_t)
    qb, kb, vb = q.astype(BF16), k.astype(BF16), v.astype(BF16)
    s = _dot_nt(qb, kb) * jnp.exp(logw - m_t)
    num = w_state * _dot(qb, ct.astype(BF16)) + _dot(s.astype(BF16), vb)
    den = (w_state * jnp.sum(q * n, axis=1, keepdims=True)
           + jnp.sum(s, axis=1, keepdims=True))
    h = num / jnp.maximum(jnp.abs(den), jnp.exp(-m_t))
    m_new = m_t[last:last + 1, :]
    b_last = b_col[last:last + 1, :]
    w_s = jnp.exp(b_last - b_col + i_col - m_new)
    a_state = jnp.exp(b_last + m - m_new)
    kw = k * w_s
    ct_new = a_state * ct + _dot_tn(kw.astype(BF16), vb)
    n_new = a_state * n + jnp.sum(kw, axis=0, keepdims=True)
    return h, ct_new, n_new, m_new


def _mlstm_heads(q_ref, k_ref, v_ref, gcol_ref, grow_ref, ct_ref, n_ref, m_ref, reverse, emit):
    dk = q_ref.shape[-1] // M_HEADS
    dv = v_ref.shape[-1] // M_HEADS
    g0 = 2 * M_HEADS if reverse else 0

    @pl.when(pl.program_id(1) == 0)
    def _():
        ct_ref[...] = jnp.zeros_like(ct_ref)
        n_ref[...] = jnp.zeros_like(n_ref)
        m_ref[...] = jnp.zeros_like(m_ref)

    for hd in range(M_HEADS):
        q = q_ref[:, hd * dk:(hd + 1) * dk] * (dk ** -0.5)
        k = k_ref[:, hd * dk:(hd + 1) * dk]
        v = v_ref[:, hd * dv:(hd + 1) * dv]
        i_col = gcol_ref[:, g0 + hd:g0 + hd + 1]
        f_col = gcol_ref[:, g0 + M_HEADS + hd:g0 + M_HEADS + hd + 1]
        i_row = grow_ref[g0 + hd:g0 + hd + 1, :]
        f_row = grow_ref[g0 + M_HEADS + hd:g0 + M_HEADS + hd + 1, :]
        h, ct_new, n_new, m_new = _mlstm_chunk(
            q, k, v, i_col, f_col, i_row, f_row, ct_ref[hd], n_ref[hd], m_ref[hd], reverse)
        ct_ref[hd] = ct_new
        n_ref[hd] = n_new
        m_ref[hd] = m_new
        emit(hd, h)


def _mlstm_fwd_kernel(q_ref, k_ref, v_ref, gcol_ref, grow_ref, h_ref, ct_ref, n_ref, m_ref):
    dv = v_ref.shape[-1] // M_HEADS

    def emit(hd, h):
        h_ref[:, hd * dv:(hd + 1) * dv] = h

    _mlstm_heads(q_ref, k_ref, v_ref, gcol_ref, grow_ref, ct_ref, n_ref, m_ref, False, emit)


def _mlstm_bwd_kernel(q_ref, k_ref, v_ref, gcol_ref, grow_ref, hf_ref, o_ref, z_ref, gm_ref,
                      out_ref, ct_ref, n_ref, m_ref):
    dv = v_ref.shape[-1] // M_HEADS

    def emit(hd, h):
        cols = slice(hd * dv, (hd + 1) * dv)
        hs = hf_ref[:, cols] + h
        hn = hs * lax.rsqrt(jnp.mean(hs * hs, axis=-1, keepdims=True) + EPS) * gm_ref[:, cols]
        z = z_ref[:, cols]
        gated = hn * jax.nn.sigmoid(o_ref[:, cols]) * (z * jax.nn.sigmoid(z))
        out_ref[:, cols] = gated.astype(out_ref.dtype)

    _mlstm_heads(q_ref, k_ref, v_ref, gcol_ref, grow_ref, ct_ref, n_ref, m_ref, True, emit)


def _mlstm(p3, gact, gact_t, g_mlstm, ctx_len):
    bsz, t_all, _ = p3.shape
    mw = g_mlstm.shape[-1]
    qw = mw // 2
    nc = t_all // CHUNK
    n_ctx = ctx_len // CHUNK
    dk, dv = qw // M_HEADS, mw // M_HEADS

    def rev_chunk(c):
        return jnp.where(c < n_ctx, n_ctx - 1 - c, nc - 1 - (c - n_ctx))

    def specs(chunk_of):
        return [
            pl.BlockSpec((None, CHUNK, qw), lambda b, c: (b, chunk_of(c), 0)),
            pl.BlockSpec((None, CHUNK, qw), lambda b, c: (b, chunk_of(c), 1)),
            pl.BlockSpec((None, CHUNK, mw), lambda b, c: (b, chunk_of(c), 1)),
            pl.BlockSpec((None, CHUNK, LANES), lambda b, c: (b, chunk_of(c), 0)),
            pl.BlockSpec((None, N_GATES, CHUNK), lambda b, c: (b, 0, chunk_of(c))),
        ]

    scratch = [pltpu.VMEM((M_HEADS, dk, dv), F32), pltpu.VMEM((M_HEADS, 1, dk), F32),
               pltpu.VMEM((M_HEADS, 1, 1), F32)]
    h_fwd = pl.pallas_call(
        _mlstm_fwd_kernel,
        grid=(bsz, nc),
        in_specs=specs(lambda c: c),
        out_specs=pl.BlockSpec((None, CHUNK, mw), lambda b, c: (b, c, 0)),
        out_shape=jax.ShapeDtypeStruct((bsz, t_all, mw), F32),
        scratch_shapes=scratch,
        compiler_params=_params("arbitrary", "arbitrary"),
        name="mlstm_fwd",
    )(p3, p3, p3, gact, gact_t)
    return pl.pallas_call(
        _mlstm_bwd_kernel,
        grid=(bsz, nc),
        in_specs=specs(rev_chunk) + [
            pl.BlockSpec((None, CHUNK, mw), lambda b, c: (b, rev_chunk(c), 0)),
            pl.BlockSpec((None, CHUNK, mw), lambda b, c: (b, rev_chunk(c), 2)),
            pl.BlockSpec((None, CHUNK, mw), lambda b, c: (b, rev_chunk(c), 3)),
            pl.BlockSpec((1, mw), lambda b, c: (0, 0)),
        ],
        out_specs=pl.BlockSpec((None, CHUNK, mw), lambda b, c: (b, rev_chunk(c), 0)),
        out_shape=jax.ShapeDtypeStruct((bsz, t_all, mw), BF16),
        scratch_shapes=scratch,
        compiler_params=_params("arbitrary", "arbitrary"),
        name="mlstm_bwd",
    )(p3, p3, p3, gact, gact_t, h_fwd, p3, p3, g_mlstm.reshape(1, mw))


def _rope_tables(seq, ctx_len):
    axis_dim = HEAD_DIM // 2
    rows = seq // GRID_W
    row_ids = jnp.repeat(jnp.arange(rows), GRID_W).astype(F32)
    col_ids = jnp.tile(jnp.arange(GRID_W), rows).astype(F32)
    inv = ROPE_THETA ** (-jnp.arange(0, axis_dim, 2, dtype=F32) / axis_dim)
    ang_r, ang_c = row_ids[:, None] * inv, col_ids[:, None] * inv
    cos_t = jnp.concatenate([jnp.cos(ang_r)] * 2 + [jnp.cos(ang_c)] * 2, axis=-1)
    sin_t = jnp.concatenate([-jnp.sin(ang_r), jnp.sin(ang_r), -jnp.sin(ang_c), jnp.sin(ang_c)], axis=-1)
    cos_t = jnp.concatenate([jnp.ones((ctx_len, HEAD_DIM), F32), cos_t], axis=0)
    sin_t = jnp.concatenate([jnp.zeros((ctx_len, HEAD_DIM), F32), sin_t], axis=0)
    return cos_t, sin_t


def _qk_prep_kernel(q_ref, k_ref, v_ref, cos_ref, sin_ref, gq_ref, gk_ref, qo_ref, ko_ref, vo_ref):
    cos_t, sin_t = cos_ref[...], sin_ref[...]
    quarter = HEAD_DIM // 4
    lane = lax.broadcasted_iota(jnp.int32, cos_t.shape, 1)
    first_half = (lane % (2 * quarter)) < quarter

    def norm_rope(x, g, post_scale):
        y = x * lax.rsqrt(jnp.mean(x * x, axis=-1, keepdims=True) + EPS) * g
        swapped = jnp.where(first_half, pltpu.roll(y, HEAD_DIM - quarter, axis=1),
                            pltpu.roll(y, quarter, axis=1))
        return ((y * cos_t + swapped * sin_t) * post_scale)

    for hd in range(q_ref.shape[-1] // HEAD_DIM):
        cols = slice(hd * HEAD_DIM, (hd + 1) * HEAD_DIM)
        qo_ref[:, cols] = norm_rope(q_ref[:, cols], gq_ref[...], HEAD_DIM ** -0.5).astype(BF16)
    for hd in range(k_ref.shape[-1] // HEAD_DIM):
        cols = slice(hd * HEAD_DIM, (hd + 1) * HEAD_DIM)
        ko_ref[:, cols] = norm_rope(k_ref[:, cols], gk_ref[...], 1.0).astype(BF16)
    vo_ref[...] = v_ref[...].astype(BF16)


def _qk_prep(p3, cos_t, sin_t, g_q, g_k, aw, ctx_len):
    bsz, t_all, _ = p3.shape
    kvw = aw // GQA_GROUP
    tm = _tile(math.gcd(ctx_len, t_all - ctx_len), 256, 16)
    q_blk = 4
    k_blk = 5 * aw // kvw
    return pl.pallas_call(
        _qk_prep_kernel,
        grid=(bsz, t_all // tm),
        in_specs=[
            pl.BlockSpec((None, tm, aw), lambda b, t: (b, t, q_blk)),
            pl.BlockSpec((None, tm, kvw), lambda b, t: (b, t, k_blk)),
            pl.BlockSpec((None, tm, kvw), lambda b, t: (b, t, k_blk + 1)),
            pl.BlockSpec((tm, HEAD_DIM), lambda b, t: (t, 0)),
            pl.BlockSpec((tm, HEAD_DIM), lambda b, t: (t, 0)),
            pl.BlockSpec((1, HEAD_DIM), lambda b, t: (0, 0)),
            pl.BlockSpec((1, HEAD_DIM), lambda b, t: (0, 0)),
        ],
        out_specs=[
            pl.BlockSpec((None, tm, aw), lambda b, t: (b, t, 0)),
            pl.BlockSpec((None, tm, kvw), lambda b, t: (b, t, 0)),
            pl.BlockSpec((None, tm, kvw), lambda b, t: (b, t, 0)),
        ],
        out_shape=[
            jax.ShapeDtypeStruct((bsz, t_all, aw), BF16),
            jax.ShapeDtypeStruct((bsz, t_all, kvw), BF16),
            jax.ShapeDtypeStruct((bsz, t_all, kvw), BF16),
        ],
        compiler_params=_params("arbitrary", "arbitrary"),
        name="qk_prep",
    )(p3, p3, p3, cos_t, sin_t, g_q.reshape(1, HEAD_DIM), g_k.reshape(1, HEAD_DIM))


def _attn_kernel(q_ref, k_ref, v_ref, z_ref, o_ref, *, ctx_len, key_chunk, update_ctx):
    tq = q_ref.shape[0]
    t_all = k_ref.shape[0]
    q = jnp.concatenate([q_ref[:, g * HEAD_DIM:(g + 1) * HEAD_DIM] for g in range(GQA_GROUP)], axis=0)

    def attend(n_keys):
        bounds = [(0, ctx_len)] if ctx_len else []
        start = ctx_len
        while start < n_keys:
            bounds.append((start, min(start + key_chunk, n_keys)))
            start += key_chunk
        m_run = l_run = acc = None
        for lo, hi in bounds:
            s = _dot_nt(q, k_ref[lo:hi, :])
            m_chunk = jnp.max(s, axis=-1, keepdims=True)
            if m_run is None:
                m_new = m_chunk
                p = jnp.exp(s - m_new)
                l_run = jnp.sum(p, axis=-1, keepdims=True)
                acc = _dot(p.astype(BF16), v_ref[lo:hi, :])
            else:
                m_new = jnp.maximum(m_run, m_chunk)
                alpha = jnp.exp(m_run - m_new)
                p = jnp.exp(s - m_new)
                l_run = alpha * l_run + jnp.sum(p, axis=-1, keepdims=True)
                acc = alpha * acc + _dot(p.astype(BF16), v_ref[lo:hi, :])
            m_run = m_new
        o = acc / l_run
        for g in range(GQA_GROUP):
            cols = slice(g * HEAD_DIM, (g + 1) * HEAD_DIM)
            z = z_ref[:, cols]
            o_ref[:, cols] = (o[g * tq:(g + 1) * tq] * (z * jax.nn.sigmoid(z))).astype(o_ref.dtype)

    @pl.when(pl.program_id(2) == 0)
    def _():
        if update_ctx:
            attend(ctx_len)
        else:
            o_ref[...] = jnp.zeros_like(o_ref)

    @pl.when(pl.program_id(2) > 0)
    def _():
        attend(t_all)


def _attention(qn, kn, vb, p3, aw, ctx_len, update_ctx):
    bsz, t_all, _ = qn.shape
    kvw = aw // GQA_GROUP
    kv_heads = kvw // HEAD_DIM
    gw = GQA_GROUP * HEAD_DIM
    tq = ctx_len
    assert (t_all - ctx_len) % tq == 0 and tq % 16 == 0
    z_blk = (5 * aw + 2 * kvw) // gw
    kern = functools.partial(_attn_kernel, ctx_len=ctx_len, key_chunk=512, update_ctx=update_ctx)
    return pl.pallas_call(
        kern,
        grid=(bsz, kv_heads, t_all // tq),
        in_specs=[
            pl.BlockSpec((None, tq, gw), lambda b, h, i: (b, i, h)),
            pl.BlockSpec((None, t_all, HEAD_DIM), lambda b, h, i: (b, 0, h)),
            pl.BlockSpec((None, t_all, HEAD_DIM), lambda b, h, i: (b, 0, h)),
            pl.BlockSpec((None, tq, gw), lambda b, h, i: (b, i, z_blk + h)),
        ],
        out_specs=pl.BlockSpec((None, tq, gw), lambda b, h, i: (b, i, h)),
        out_shape=jax.ShapeDtypeStruct((bsz, t_all, aw), BF16),
        compiler_params=_params("arbitrary", "arbitrary", "arbitrary"),
        name="attention",
    )(qn, kn, vb, p3)


def _out_proj_kernel(am_ref, aa_ref, wm_ref, wa_ref, x_ref, gl_ref, gc_ref, o_ref, *, ctx_len, tiles_per_batch):
    tm = x_ref.shape[0]
    y = _dot(am_ref[...], wm_ref[...]) + _dot(aa_ref[...], wa_ref[...])
    row0 = (pl.program_id(0) % tiles_per_batch) * tm
    row = row0 + lax.broadcasted_iota(jnp.int32, y.shape, 0)
    gate = jnp.where(row < ctx_len, gc_ref[...], gl_ref[...])
    o_ref[...] = x_ref[...] + gate * y


def _out_projection(m_out, a_out, w_out, x_all, mod_rows, ctx_len):
    bsz, t_all, d = x_all.shape
    mw = m_out.shape[-1]
    aw = a_out.shape[-1]
    assert mw == aw
    tm = _tile(t_all, 1152, 16)
    tpb = t_all // tm
    tn = _tile(d, 512, LANES)
    m = bsz * t_all
    kern = functools.partial(_out_proj_kernel, ctx_len=ctx_len, tiles_per_batch=tpb)
    out = pl.pallas_call(
        kern,
        grid=(m // tm, d // tn),
        in_specs=[
            pl.BlockSpec((tm, mw), lambda i, j: (i, 0)),
            pl.BlockSpec((tm, aw), lambda i, j: (i, 0)),
            pl.BlockSpec((mw, tn), lambda i, j: (0, j)),
            pl.BlockSpec((aw, tn), lambda i, j: (1, j)),
            pl.BlockSpec((tm, tn), lambda i, j: (i, j)),
            pl.BlockSpec((None, None, 1, tn), lambda i, j: (i // tpb, 2, 0, j)),
            pl.BlockSpec((None, None, 1, tn), lambda i, j: (bsz, 2, 0, j)),
        ],
        out_specs=pl.BlockSpec((tm, tn), lambda i, j: (i, j)),
        out_shape=jax.ShapeDtypeStruct((m, d), F32),
        compiler_params=_params("arbitrary", "arbitrary"),
        name="out_projection",
    )(m_out.reshape(m, mw), a_out.reshape(m, aw), w_out, w_out, x_all.reshape(m, d), mod_rows, mod_rows)
    return out.reshape(bsz, t_all, d)


def _final_norm_kernel(x_ref, g_ref, o_ref):
    x = x_ref[...]
    o_ref[...] = x * lax.rsqrt(jnp.mean(x * x, axis=-1, keepdims=True) + EPS) * g_ref[...]


def _final_norm(x_all, g_final, ctx_len):
    bsz, t_all, d = x_all.shape
    seq = t_all - ctx_len
    tm = _tile(math.gcd(ctx_len, seq), 256, 8)
    off = ctx_len // tm
    return pl.pallas_call(
        _final_norm_kernel,
        grid=(bsz, seq // tm),
        in_specs=[
            pl.BlockSpec((None, tm, d), lambda b, t: (b, t + off, 0)),
            pl.BlockSpec((1, d), lambda b, t: (0, 0)),
        ],
        out_specs=pl.BlockSpec((None, tm, d), lambda b, t: (b, t, 0)),
        out_shape=jax.ShapeDtypeStruct((bsz, seq, d), F32),
        compiler_params=_params("arbitrary", "arbitrary"),
        name="final_norm",
    )(x_all, g_final.reshape(1, d))


def kernel(x, c, ctx, c_ctx, w_mod, b_mod, g_norm, w_in, b_gate, g_mlstm, g_q, g_k, w_out, g_final):
    bsz, seq, d = x.shape
    ctx_len = ctx.shape[1]
    depth = w_mod.shape[0]
    mw = g_mlstm.shape[-1]
    aw = w_out.shape[1] - mw
    gate_col = 4 * mw
    assert mw == aw and w_in.shape[-1] == gate_col + N_GATES + aw * 5 // 2
    assert ctx_len % CHUNK == 0 and seq % CHUNK == 0 and seq % GRID_W == 0

    x_all = jnp.concatenate([ctx, x], axis=1)
    w_main = jnp.concatenate([w_in[:, :, :gate_col], w_in[:, :, gate_col + N_GATES:]], axis=-1).astype(BF16)
    w_gate = jnp.pad(w_in[:, :, gate_col:gate_col + N_GATES], ((0, 0), (0, 0), (0, LANES - N_GATES))).astype(BF16)
    b_gate_p = jnp.pad(b_gate, ((0, 0), (0, LANES - N_GATES))).reshape(depth, 1, LANES)
    w_out_b = w_out.astype(BF16)
    cos_t, sin_t = _rope_tables(seq, ctx_len)

    c_rows = jnp.concatenate([c, c_ctx[None, :]], axis=0)
    n_rows = bsz + 1
    c_rows = jnp.pad(c_rows, ((0, -n_rows % 8), (0, 0)))
    mod = _modulation(c_rows, w_mod, b_mod)
    mod = mod[:, :n_rows].reshape(depth, n_rows, 3, 1, d)

    for layer in range(depth):
        update_ctx = layer < depth - 1
        h, gact = _prologue(x_all, mod[layer], g_norm[layer], w_gate[layer], b_gate_p[layer], ctx_len)
        p = _in_projection(h.reshape(bsz * (ctx_len + seq), d), w_main[layer])
        p3 = p.reshape(bsz, ctx_len + seq, -1)
        gact_t = jnp.swapaxes(gact[:, :, :N_GATES], 1, 2)
        m_out = _mlstm(p3, gact, gact_t, g_mlstm[layer], ctx_len)
        qn, kn, vb = _qk_prep(p3, cos_t, sin_t, g_q[layer], g_k[layer], aw, ctx_len)
        a_out = _attention(qn, kn, vb, p3, aw, ctx_len, update_ctx)
        x_all = _out_projection(m_out, a_out, w_out_b[layer], x_all, mod[layer], ctx_len)
    return _final_norm(x_all, g_final, ctx_len)
```

```python
import functools
import math

import jax
import jax.numpy as jnp
from jax import lax
from jax.experimental import pallas as pl
from jax.experimental.pallas import tpu as pltpu

CHUNK = 128
M_HEADS = 4
HEAD_DIM = 128
GQA_GROUP = 4
GRID_W = 64
ROPE_THETA = 10000.0
EPS = 1e-6
N_GATES = 4 * M_HEADS
LOG2_E = 1.4426950408889634
SOFTMAX_ROWS = 128

LANES = 128
V7X_VMEM_LIMIT_BYTES = 56 * 1024 * 1024

F32 = jnp.float32
BF16 = jnp.bfloat16


def _params(*sem):
    return pltpu.CompilerParams(dimension_semantics=sem, vmem_limit_bytes=V7X_VMEM_LIMIT_BYTES)


def _tile(total, target, multiple):
    best = None
    for t in range(multiple, min(total, target) + 1, multiple):
        if total % t == 0:
            best = t
    assert best is not None, (total, target, multiple)
    return best


def _dot(a, b):
    return jnp.dot(a, b, preferred_element_type=F32)


def _dot_nt(a, b):
    return lax.dot_general(a, b, (((1,), (1,)), ((), ())), preferred_element_type=F32)


def _dot_tn(a, b):
    return lax.dot_general(a, b, (((0,), (0,)), ((), ())), preferred_element_type=F32)


def _mod_kernel(c_ref, w_ref, b_ref, o_ref):
    @pl.when(pl.program_id(1) == 0)
    def _():
        o_ref[...] = jnp.broadcast_to(b_ref[...], o_ref.shape)

    c = c_ref[...]
    a = (c * jax.nn.sigmoid(c)).astype(BF16)
    o_ref[...] += _dot(a, w_ref[...].astype(BF16))


def _modulation(c_rows, w_mod, b_mod):
    depth, d, n = w_mod.shape
    rows = c_rows.shape[0]
    tk = _tile(d, 256, LANES)
    c_chunks = c_rows.reshape(rows, d // tk, tk).swapaxes(0, 1)
    return pl.pallas_call(
        _mod_kernel,
        grid=(depth, d // tk),
        in_specs=[
            pl.BlockSpec((None, rows, tk), lambda l, k: (k, 0, 0)),
            pl.BlockSpec((None, tk, n), lambda l, k: (l, k, 0)),
            pl.BlockSpec((None, 1, n), lambda l, k: (l, 0, 0)),
        ],
        out_specs=pl.BlockSpec((None, rows, n), lambda l, k: (l, 0, 0)),
        out_shape=jax.ShapeDtypeStruct((depth, rows, n), F32),
        compiler_params=_params("arbitrary", "arbitrary"),
        name="modulation",
    )(c_chunks, w_mod, b_mod.reshape(depth, 1, n))


def _prologue_first_kernel(ctx_ref, lat_ref, g_ref, scale_ref, shift_ref, wg_ref, bg_ref,
                           h_ref, gact_ref, xall_ref, *, n_ctx_tiles):
    @pl.when(pl.program_id(1) < n_ctx_tiles)
    def _():
        xall_ref[...] = ctx_ref[...]

    @pl.when(pl.program_id(1) >= n_ctx_tiles)
    def _():
        xall_ref[...] = lat_ref[...]

    _prologue_kernel(xall_ref, g_ref, scale_ref, shift_ref, wg_ref, bg_ref, h_ref, gact_ref)


def _prologue_kernel(x_ref, g_ref, scale_ref, shift_ref, wg_ref, bg_ref, h_ref, gact_ref):
    x = x_ref[...]
    y = x * lax.rsqrt(jnp.mean(x * x, axis=-1, keepdims=True) + EPS)
    h = (y * g_ref[...]) * (1.0 + scale_ref[...]) + shift_ref[...]
    hb = h.astype(BF16)
    h_ref[...] = hb
    pre = _dot(hb, wg_ref[...]) + bg_ref[...]
    col = lax.broadcasted_iota(jnp.int32, pre.shape, 1)
    is_forget = (col & M_HEADS) != 0
    log_sig = jnp.minimum(pre, 0.0) - jnp.log1p(jnp.exp(-jnp.abs(pre)))
    gact_ref[...] = jnp.where(is_forget, log_sig, pre)


def _prologue(tokens, mod_rows, g_norm, w_gate, b_gate, ctx_len):
    first = isinstance(tokens, tuple)
    if first:
        ctx, lat = tokens
        bsz, seq, d = lat.shape
        t_all = ctx_len + seq
    else:
        bsz, t_all, d = tokens.shape
    tm = _tile(math.gcd(ctx_len, t_all - ctx_len), 256, 16)
    n_ctx_tiles = ctx_len // tm

    def mod_row(b, t):
        return jnp.where(t < n_ctx_tiles, bsz, b)

    tile_spec = pl.BlockSpec((None, tm, d), lambda b, t: (b, t, 0))
    if first:
        kern = functools.partial(_prologue_first_kernel, n_ctx_tiles=n_ctx_tiles)
        token_specs = [
            pl.BlockSpec((None, tm, d), lambda b, t: (b, jnp.minimum(t, n_ctx_tiles - 1), 0)),
            pl.BlockSpec((None, tm, d), lambda b, t: (b, jnp.maximum(t - n_ctx_tiles, 0), 0)),
        ]
        token_args = [ctx, lat]
    else:
        kern, token_specs, token_args = _prologue_kernel, [tile_spec], [tokens]
    out_specs = [tile_spec, pl.BlockSpec((None, tm, LANES), lambda b, t: (b, t, 0))]
    out_shape = [jax.ShapeDtypeStruct((bsz, t_all, d), BF16), jax.ShapeDtypeStruct((bsz, t_all, LANES), F32)]
    if first:
        out_specs.append(tile_spec)
        out_shape.append(jax.ShapeDtypeStruct((bsz, t_all, d), F32))
    return pl.pallas_call(
        kern,
        grid=(bsz, t_all // tm),
        in_specs=token_specs + [
            pl.BlockSpec((1, d), lambda b, t: (0, 0)),
            pl.BlockSpec((None, None, 1, d), lambda b, t: (mod_row(b, t), 1, 0, 0)),
            pl.BlockSpec((None, None, 1, d), lambda b, t: (mod_row(b, t), 0, 0, 0)),
            pl.BlockSpec((d, LANES), lambda b, t: (0, 0)),
            pl.BlockSpec((1, LANES), lambda b, t: (0, 0)),
        ],
        out_specs=out_specs,
        out_shape=out_shape,
        compiler_params=_params("arbitrary", "arbitrary"),
        name="prologue",
    )(*token_args, g_norm.reshape(1, d), mod_rows, mod_rows, w_gate, b_gate)


def _matmul_kernel(a_ref, w_ref, o_ref):
    o_ref[...] = _dot(a_ref[...], w_ref[...])


def _in_projection(h2d, w):
    m, k = h2d.shape
    n = w.shape[1]
    tm = _tile(m, 1152, 16)
    tn = _tile(n, 1024, LANES)
    return pl.pallas_call(
        _matmul_kernel,
        grid=(m // tm, n // tn),
        in_specs=[
            pl.BlockSpec((tm, k), lambda i, j: (i, 0)),
            pl.BlockSpec((k, tn), lambda i, j: (0, j)),
        ],
        out_specs=pl.BlockSpec((tm, tn), lambda i, j: (i, j)),
        out_shape=jax.ShapeDtypeStruct((m, n), F32),
        compiler_params=_params("arbitrary", "arbitrary"),
        name="in_projection",
    )(h2d, w)


def _matmul_f32w_kernel(a_ref, w_ref, o_ref, wb_ref):
    @pl.when(pl.program_id(1) == 0)
    def _():
        wb_ref[...] = w_ref[...].astype(BF16)

    o_ref[...] = _dot(a_ref[...], wb_ref[...])


def _in_projection_f32w(h2d, w_layers, layer, n):
    m, k = h2d.shape
    tm = _tile(m, 1152, 16)
    tn = _tile(n, 512, LANES)
    return pl.pallas_call(
        _matmul_f32w_kernel,
        grid=(n // tn, m // tm),
        in_specs=[
            pl.BlockSpec((tm, k), lambda j, i: (i, 0)),
            pl.BlockSpec((None, k, tn), lambda j, i: (layer, 0, j)),
        ],
        out_specs=pl.BlockSpec((tm, tn), lambda j, i: (i, j)),
        out_shape=jax.ShapeDtypeStruct((m, n), F32),
        scratch_shapes=[pltpu.VMEM((k, tn), BF16)],
        compiler_params=_params("arbitrary", "arbitrary"),
        name="in_projection_f32w",
    )(h2d, w_layers)


def _mlstm_chunk(q, k, v, i_col, f_col, i_row, f_row, ct, n, m, reverse):
    L = q.shape[0]
    t_idx = lax.broadcasted_iota(jnp.int32, (L, L), 0)
    s_idx = lax.broadcasted_iota(jnp.int32, (L, L), 1)
    if reverse:
        seen = s_idx >= t_idx
        seen_t = t_idx >= s_idx
        last = 0
    else:
        seen = s_idx <= t_idx
        seen_t = t_idx <= s_idx
        last = L - 1
    b_col = jnp.sum(jnp.where(seen, f_row, 0.0), axis=1, keepdims=True)
    b_row = jnp.sum(jnp.where(seen_t, f_col, 0.0), axis=0, keepdims=True)
    logw = jnp.where(seen, b_col - b_row + i_row, -jnp.inf)
    m_state = b_col + m
    m_t = jnp.maximum(m_state, jnp.max(logw, axis=1, keepdims=True))
    w_state = jnp.exp(m_state - m_t)
    qb, kb, vb = q.astype(BF16), k.astype(BF16), v.astype(BF16)
    s = _dot_nt(qb, kb) * jnp.exp(logw - m_t)
    num = w_state * _dot(qb, ct.astype(BF16)) + _dot(s.astype(BF16), vb)
    den = (w_state * jnp.sum(q * n, axis=1, keepdims=True)
           + jnp.sum(s, axis=1, keepdims=True))
    h = num / jnp.maximum(jnp.abs(den), jnp.exp(-m_t))
    m_new = m_t[last:last + 1, :]
    b_last = b_col[last:last + 1, :]
    w_s = jnp.exp(b_last - b_col + i_col - m_new)
    a_state = jnp.exp(b_last + m - m_new)
    kw = k * w_s
    ct_new = a_state * ct + _dot_tn(kw.astype(BF16), vb)
    n_new = a_state * n + jnp.sum(kw, axis=0, keepdims=True)
    return h, ct_new, n_new, m_new


def _mlstm_heads(q_ref, k_ref, v_ref, gcol_ref, grow_ref, ct_ref, n_ref, m_ref, reverse, emit):
    dk = q_ref.shape[-1] // M_HEADS
    dv = v_ref.shape[-1] // M_HEADS
    g0 = 2 * M_HEADS if reverse else 0

    @pl.when(pl.program_id(1) == 0)
    def _():
        ct_ref[...] = jnp.zeros_like(ct_ref)
        n_ref[...] = jnp.zeros_like(n_ref)
        m_ref[...] = jnp.zeros_like(m_ref)

    for hd in range(M_HEADS):
        q = q_ref[:, hd * dk:(hd + 1) * dk] * (dk ** -0.5)
        k = k_ref[:, hd * dk:(hd + 1) * dk]
        v = v_ref[:, hd * dv:(hd + 1) * dv]
        i_col = gcol_ref[:, g0 + hd:g0 + hd + 1]
        f_col = gcol_ref[:, g0 + M_HEADS + hd:g0 + M_HEADS + hd + 1]
        i_row = grow_ref[g0 + hd:g0 + hd + 1, :]
        f_row = grow_ref[g0 + M_HEADS + hd:g0 + M_HEADS + hd + 1, :]
        h, ct_new, n_new, m_new = _mlstm_chunk(
            q, k, v, i_col, f_col, i_row, f_row, ct_ref[hd], n_ref[hd], m_ref[hd], reverse)
        ct_ref[hd] = ct_new
        n_ref[hd] = n_new
        m_ref[hd] = m_new
        emit(hd, h)


def _mlstm_fwd_kernel(q_ref, k_ref, v_ref, gcol_ref, grow_ref, h_ref, ct_ref, n_ref, m_ref):
    dv = v_ref.shape[-1] // M_HEADS

    def emit(hd, h):
        h_ref[:, hd * dv:(hd + 1) * dv] = h

    _mlstm_heads(q_ref, k_ref, v_ref, gcol_ref, grow_ref, ct_ref, n_ref, m_ref, False, emit)


def _mlstm_bwd_kernel(q_ref, k_ref, v_ref, gcol_ref, grow_ref, hf_ref, o_ref, z_ref, gm_ref,
                      out_ref, ct_ref, n_ref, m_ref):
    dv = v_ref.shape[-1] // M_HEADS

    def emit(hd, h):
        cols = slice(hd * dv, (hd + 1) * dv)
        hs = hf_ref[:, cols] + h
        hn = hs * lax.rsqrt(jnp.mean(hs * hs, axis=-1, keepdims=True) + EPS) * gm_ref[:, cols]
        z = z_ref[:, cols]
        gated = hn * jax.nn.sigmoid(o_ref[:, cols]) * (z * jax.nn.sigmoid(z))
        out_ref[:, cols] = gated.astype(out_ref.dtype)

    _mlstm_heads(q_ref, k_ref, v_ref, gcol_ref, grow_ref, ct_ref, n_ref, m_ref, True, emit)


def _mlstm(p3, gact, gact_t, g_mlstm, ctx_len):
    bsz, t_all, _ = p3.shape
    mw = g_mlstm.shape[-1]
    qw = mw // 2
    nc = t_all // CHUNK
    n_ctx = ctx_len // CHUNK
    dk, dv = qw // M_HEADS, mw // M_HEADS

    def rev_chunk(c):
        return jnp.where(c < n_ctx, n_ctx - 1 - c, nc - 1 - (c - n_ctx))

    def specs(chunk_of):
        return [
            pl.BlockSpec((None, CHUNK, qw), lambda b, c: (b, chunk_of(c), 0)),
            pl.BlockSpec((None, CHUNK, qw), lambda b, c: (b, chunk_of(c), 1)),
            pl.BlockSpec((None, CHUNK, mw), lambda b, c: (b, chunk_of(c), 1)),
            pl.BlockSpec((None, CHUNK, LANES), lambda b, c: (b, chunk_of(c), 0)),
            pl.BlockSpec((None, N_GATES, CHUNK), lambda b, c: (b, 0, chunk_of(c))),
        ]

    scratch = [pltpu.VMEM((M_HEADS, dk, dv), F32), pltpu.VMEM((M_HEADS, 1, dk), F32),
               pltpu.VMEM((M_HEADS, 1, 1), F32)]
    h_fwd = pl.pallas_call(
        _mlstm_fwd_kernel,
        grid=(bsz, nc),
        in_specs=specs(lambda c: c),
        out_specs=pl.BlockSpec((None, CHUNK, mw), lambda b, c: (b, c, 0)),
        out_shape=jax.ShapeDtypeStruct((bsz, t_all, mw), F32),
        scratch_shapes=scratch,
        compiler_params=_params("arbitrary", "arbitrary"),
        name="mlstm_fwd",
    )(p3, p3, p3, gact, gact_t)
    return pl.pallas_call(
        _mlstm_bwd_kernel,
        grid=(bsz, nc),
        in_specs=specs(rev_chunk) + [
            pl.BlockSpec((None, CHUNK, mw), lambda b, c: (b, rev_chunk(c), 0)),
            pl.BlockSpec((None, CHUNK, mw), lambda b, c: (b, rev_chunk(c), 2)),
            pl.BlockSpec((None, CHUNK, mw), lambda b, c: (b, rev_chunk(c), 3)),
            pl.BlockSpec((1, mw), lambda b, c: (0, 0)),
        ],
        out_specs=pl.BlockSpec((None, CHUNK, mw), lambda b, c: (b, rev_chunk(c), 0)),
        out_shape=jax.ShapeDtypeStruct((bsz, t_all, mw), BF16),
        scratch_shapes=scratch,
        compiler_params=_params("arbitrary", "arbitrary"),
        name="mlstm_bwd",
    )(p3, p3, p3, gact, gact_t, h_fwd, p3, p3, g_mlstm.reshape(1, mw))


def _rope_tables(seq, ctx_len):
    axis_dim = HEAD_DIM // 2
    rows = seq // GRID_W
    row_ids = jnp.repeat(jnp.arange(rows), GRID_W).astype(F32)
    col_ids = jnp.tile(jnp.arange(GRID_W), rows).astype(F32)
    inv = ROPE_THETA ** (-jnp.arange(0, axis_dim, 2, dtype=F32) / axis_dim)
    ang_r, ang_c = row_ids[:, None] * inv, col_ids[:, None] * inv
    cos_t = jnp.concatenate([jnp.cos(ang_r)] * 2 + [jnp.cos(ang_c)] * 2, axis=-1)
    sin_t = jnp.concatenate([-jnp.sin(ang_r), jnp.sin(ang_r), -jnp.sin(ang_c), jnp.sin(ang_c)], axis=-1)
    cos_t = jnp.concatenate([jnp.ones((ctx_len, HEAD_DIM), F32), cos_t], axis=0)
    sin_t = jnp.concatenate([jnp.zeros((ctx_len, HEAD_DIM), F32), sin_t], axis=0)
    return cos_t, sin_t


def _qk_prep_kernel(q_ref, k_ref, v_ref, cos_ref, sin_ref, gq_ref, gk_ref, qo_ref, ko_ref, vo_ref):
    cos_t, sin_t = cos_ref[...], sin_ref[...]
    quarter = HEAD_DIM // 4
    lane = lax.broadcasted_iota(jnp.int32, cos_t.shape, 1)
    first_half = (lane % (2 * quarter)) < quarter

    def norm_rope(x, g, post_scale):
        y = x * lax.rsqrt(jnp.mean(x * x, axis=-1, keepdims=True) + EPS) * g
        swapped = jnp.where(first_half, pltpu.roll(y, HEAD_DIM - quarter, axis=1),
                            pltpu.roll(y, quarter, axis=1))
        return ((y * cos_t + swapped * sin_t) * post_scale)

    for hd in range(q_ref.shape[-1] // HEAD_DIM):
        cols = slice(hd * HEAD_DIM, (hd + 1) * HEAD_DIM)
        qo_ref[:, cols] = norm_rope(q_ref[:, cols], gq_ref[...], HEAD_DIM ** -0.5 * LOG2_E).astype(BF16)
    for hd in range(k_ref.shape[-1] // HEAD_DIM):
        cols = slice(hd * HEAD_DIM, (hd + 1) * HEAD_DIM)
        ko_ref[:, cols] = norm_rope(k_ref[:, cols], gk_ref[...], 1.0).astype(BF16)
        vo_ref[:, 2 * hd * HEAD_DIM:(2 * hd + 1) * HEAD_DIM] = v_ref[:, cols].astype(BF16)
        vo_ref[:, (2 * hd + 1) * HEAD_DIM:(2 * hd + 2) * HEAD_DIM] = jnp.ones((v_ref.shape[0], HEAD_DIM), BF16)


def _qk_prep(p3, cos_t, sin_t, g_q, g_k, aw, ctx_len):
    bsz, t_all, _ = p3.shape
    kvw = aw // GQA_GROUP
    tm = _tile(math.gcd(ctx_len, t_all - ctx_len), 256, 16)
    q_blk = 0
    k_blk = aw // kvw
    return pl.pallas_call(
        _qk_prep_kernel,
        grid=(bsz, t_all // tm),
        in_specs=[
            pl.BlockSpec((None, tm, aw), lambda b, t: (b, t, q_blk)),
            pl.BlockSpec((None, tm, kvw), lambda b, t: (b, t, k_blk)),
            pl.BlockSpec((None, tm, kvw), lambda b, t: (b, t, k_blk + 1)),
            pl.BlockSpec((tm, HEAD_DIM), lambda b, t: (t, 0)),
            pl.BlockSpec((tm, HEAD_DIM), lambda b, t: (t, 0)),
            pl.BlockSpec((1, HEAD_DIM), lambda b, t: (0, 0)),
            pl.BlockSpec((1, HEAD_DIM), lambda b, t: (0, 0)),
        ],
        out_specs=[
            pl.BlockSpec((None, tm, aw), lambda b, t: (b, t, 0)),
            pl.BlockSpec((None, tm, kvw), lambda b, t: (b, t, 0)),
            pl.BlockSpec((None, tm, 2 * kvw), lambda b, t: (b, t, 0)),
        ],
        out_shape=[
            jax.ShapeDtypeStruct((bsz, t_all, aw), BF16),
            jax.ShapeDtypeStruct((bsz, t_all, kvw), BF16),
            jax.ShapeDtypeStruct((bsz, t_all, 2 * kvw), BF16),
        ],
        compiler_params=_params("arbitrary", "arbitrary"),
        name="qk_prep",
    )(p3, p3, p3, cos_t, sin_t, g_q.reshape(1, HEAD_DIM), g_k.reshape(1, HEAD_DIM))


def _attn_kernel(q_ref, k_ref, v_ref, z_ref, o_ref, s_ref, p_ref, *, ctx_len, key_chunk, update_ctx):
    tq = q_ref.shape[0]
    t_all = k_ref.shape[0]
    n_split = 2
    heads_per_split = GQA_GROUP // n_split
    rows_per_split = heads_per_split * tq

    def attend(n_keys):
        n_tiles = n_keys // LANES
        for sp in range(n_split):
            q = jnp.concatenate([q_ref[:, g * HEAD_DIM:(g + 1) * HEAD_DIM]
                                 for g in range(sp * heads_per_split, (sp + 1) * heads_per_split)], axis=0)
            rows = slice(sp * rows_per_split, (sp + 1) * rows_per_split)
            for lo in range(0, n_keys, key_chunk):
                hi = min(lo + key_chunk, n_keys)
                s_ref[rows, lo:hi] = _dot_nt(q, k_ref[lo:hi, :])
        for sp in range(n_split):
            for rb in range(rows_per_split // SOFTMAX_ROWS):
                r0 = sp * rows_per_split + rb * SOFTMAX_ROWS
                rows = slice(r0, r0 + SOFTMAX_ROWS)
                m_lanes = s_ref[rows, 0:LANES]
                for j in range(1, n_tiles):
                    m_lanes = jnp.maximum(m_lanes, s_ref[rows, j * LANES:(j + 1) * LANES])
                m_rows = jnp.broadcast_to(jnp.max(m_lanes, axis=-1, keepdims=True), (SOFTMAX_ROWS, LANES))
                for j in range(n_tiles):
                    cols = slice(j * LANES, (j + 1) * LANES)
                    p_ref[rows, cols] = jnp.exp2(s_ref[rows, cols] - m_rows).astype(BF16)
            rows = slice(sp * rows_per_split, (sp + 1) * rows_per_split)
            ov = _dot(p_ref[rows, 0:n_keys], v_ref[0:n_keys, :])
            o = ov[:, :HEAD_DIM] / ov[:, HEAD_DIM:HEAD_DIM + 1]
            for gl in range(heads_per_split):
                g = sp * heads_per_split + gl
                cols = slice(g * HEAD_DIM, (g + 1) * HEAD_DIM)
                z = z_ref[:, cols]
                o_ref[:, cols] = (o[gl * tq:(gl + 1) * tq] * (z * jax.nn.sigmoid(z))).astype(o_ref.dtype)

    @pl.when(pl.program_id(2) == 0)
    def _():
        if update_ctx:
            attend(ctx_len)
        else:
            o_ref[...] = jnp.zeros_like(o_ref)

    @pl.when(pl.program_id(2) > 0)
    def _():
        attend(t_all)


def _attention(qn, kn, vb, p3, aw, ctx_len, update_ctx):
    bsz, t_all, _ = qn.shape
    kvw = aw // GQA_GROUP
    kv_heads = kvw // HEAD_DIM
    gw = GQA_GROUP * HEAD_DIM
    tq = ctx_len
    assert (t_all - ctx_len) % tq == 0 and tq % 16 == 0
    z_blk = (aw + 2 * kvw) // gw
    kern = functools.partial(_attn_kernel, ctx_len=ctx_len, key_chunk=512, update_ctx=update_ctx)
    return pl.pallas_call(
        kern,
        grid=(bsz, kv_heads, t_all // tq),
        in_specs=[
            pl.BlockSpec((None, tq, gw), lambda b, h, i: (b, i, h)),
            pl.BlockSpec((None, t_all, HEAD_DIM), lambda b, h, i: (b, 0, h)),
            pl.BlockSpec((None, t_all, 2 * HEAD_DIM), lambda b, h, i: (b, 0, h)),
            pl.BlockSpec((None, tq, gw), lambda b, h, i: (b, i, z_blk + h)),
        ],
        out_specs=pl.BlockSpec((None, tq, gw), lambda b, h, i: (b, i, h)),
        out_shape=jax.ShapeDtypeStruct((bsz, t_all, aw), BF16),
        scratch_shapes=[pltpu.VMEM((GQA_GROUP * tq, t_all), F32), pltpu.VMEM((GQA_GROUP * tq, t_all), BF16)],
        compiler_params=_params("arbitrary", "arbitrary", "arbitrary"),
        name="attention",
    )(qn, kn, vb, p3)


def _out_proj_kernel(am_ref, aa_ref, wm_ref, wa_ref, x_ref, gl_ref, gc_ref, o_ref, *, ctx_len, tiles_per_batch):
    tm = x_ref.shape[0]
    y = _dot(am_ref[...], wm_ref[...]) + _dot(aa_ref[...], wa_ref[...])
    row0 = (pl.program_id(0) % tiles_per_batch) * tm
    row = row0 + lax.broadcasted_iota(jnp.int32, y.shape, 0)
    gate = jnp.where(row < ctx_len, gc_ref[...], gl_ref[...])
    o_ref[...] = x_ref[...] + gate * y


def _out_projection(m_out, a_out, w_out, x_all, mod_rows, ctx_len):
    bsz, t_all, d = x_all.shape
    mw = m_out.shape[-1]
    aw = a_out.shape[-1]
    assert mw == aw
    tm = _tile(t_all, 1152, 16)
    tpb = t_all // tm
    tn = _tile(d, 512, LANES)
    m = bsz * t_all
    kern = functools.partial(_out_proj_kernel, ctx_len=ctx_len, tiles_per_batch=tpb)
    out = pl.pallas_call(
        kern,
        grid=(m // tm, d // tn),
        in_specs=[
            pl.BlockSpec((tm, mw), lambda i, j: (i, 0)),
            pl.BlockSpec((tm, aw), lambda i, j: (i, 0)),
            pl.BlockSpec((mw, tn), lambda i, j: (0, j)),
            pl.BlockSpec((aw, tn), lambda i, j: (1, j)),
            pl.BlockSpec((tm, tn), lambda i, j: (i, j)),
            pl.BlockSpec((None, None, 1, tn), lambda i, j: (i // tpb, 2, 0, j)),
            pl.BlockSpec((None, None, 1, tn), lambda i, j: (bsz, 2, 0, j)),
        ],
        out_specs=pl.BlockSpec((tm, tn), lambda i, j: (i, j)),
        out_shape=jax.ShapeDtypeStruct((m, d), F32),
        compiler_params=_params("arbitrary", "arbitrary"),
        name="out_projection",
    )(m_out.reshape(m, mw), a_out.reshape(m, aw), w_out, w_out, x_all.reshape(m, d), mod_rows, mod_rows)
    return out.reshape(bsz, t_all, d)


def _final_norm_kernel(x_ref, g_ref, o_ref):
    x = x_ref[...]
    o_ref[...] = x * lax.rsqrt(jnp.mean(x * x, axis=-1, keepdims=True) + EPS) * g_ref[...]


def _final_norm(x_all, g_final, ctx_len):
    bsz, t_all, d = x_all.shape
    seq = t_all - ctx_len
    tm = _tile(math.gcd(ctx_len, seq), 256, 8)
    off = ctx_len // tm
    return pl.pallas_call(
        _final_norm_kernel,
        grid=(bsz, seq // tm),
        in_specs=[
            pl.BlockSpec((None, tm, d), lambda b, t: (b, t + off, 0)),
            pl.BlockSpec((1, d), lambda b, t: (0, 0)),
        ],
        out_specs=pl.BlockSpec((None, tm, d), lambda b, t: (b, t, 0)),
        out_shape=jax.ShapeDtypeStruct((bsz, seq, d), F32),
        compiler_params=_params("arbitrary", "arbitrary"),
        name="final_norm",
    )(x_all, g_final.reshape(1, d))


def kernel(x, c, ctx, c_ctx, w_mod, b_mod, g_norm, w_in, b_gate, g_mlstm, g_q, g_k, w_out, g_final):
    bsz, seq, d = x.shape
    ctx_len = ctx.shape[1]
    depth = w_mod.shape[0]
    mw = g_mlstm.shape[-1]
    aw = w_out.shape[1] - mw
    gate_col = 4 * mw
    assert mw == aw and w_in.shape[-1] == gate_col + N_GATES + aw * 5 // 2
    assert ctx_len % CHUNK == 0 and seq % CHUNK == 0 and seq % GRID_W == 0

    w_attn = w_in[:, :, gate_col + N_GATES:].astype(BF16)
    w_gate = jnp.pad(w_in[:, :, gate_col:gate_col + N_GATES], ((0, 0), (0, 0), (0, LANES - N_GATES))).astype(BF16)
    b_gate_p = jnp.pad(b_gate, ((0, 0), (0, LANES - N_GATES))).reshape(depth, 1, LANES)
    w_out_b = w_out.astype(BF16)
    cos_t, sin_t = _rope_tables(seq, ctx_len)

    c_rows = jnp.concatenate([c, c_ctx[None, :]], axis=0)
    n_rows = bsz + 1
    c_rows = jnp.pad(c_rows, ((0, -n_rows % 8), (0, 0)))
    mod = _modulation(c_rows, w_mod, b_mod)
    mod = mod[:, :n_rows].reshape(depth, n_rows, 3, 1, d)

    t_all = ctx_len + seq
    x_all = None
    for layer in range(depth):
        update_ctx = layer < depth - 1
        prologue_args = (mod[layer], g_norm[layer], w_gate[layer], b_gate_p[layer], ctx_len)
        if layer == 0:
            h, gact, x_all = _prologue((ctx, x), *prologue_args)
        else:
            h, gact = _prologue(x_all, *prologue_args)
        h2d = h.reshape(bsz * t_all, d)
        pm3 = _in_projection_f32w(h2d, w_in, layer, gate_col).reshape(bsz, t_all, gate_col)
        pa3 = _in_projection(h2d, w_attn[layer]).reshape(bsz, t_all, -1)
        gact_t = jnp.swapaxes(gact[:, :, :N_GATES], 1, 2)
        m_out = _mlstm(pm3, gact, gact_t, g_mlstm[layer], ctx_len)
        qn, kn, vb = _qk_prep(pa3, cos_t, sin_t, g_q[layer], g_k[layer], aw, ctx_len)
        a_out = _attention(qn, kn, vb, pa3, aw, ctx_len, update_ctx)
        x_all = _out_projection(m_out, a_out, w_out_b[layer], x_all, mod[layer], ctx_len)
    return _final_norm(x_all, g_final, ctx_len)
```

```python
import functools
import math

import jax
import jax.numpy as jnp
from jax import lax
from jax.experimental import pallas as pl
from jax.experimental.pallas import tpu as pltpu

CHUNK = 128
M_HEADS = 4
HEAD_DIM = 128
GQA_GROUP = 4
GRID_W = 64
ROPE_THETA = 10000.0
EPS = 1e-6
N_GATES = 4 * M_HEADS
LOG2_E = 1.4426950408889634
SOFTMAX_ROWS = 128

LANES = 128
V7X_VMEM_LIMIT_BYTES = 56 * 1024 * 1024

F32 = jnp.float32
BF16 = jnp.bfloat16


def _params(*sem):
    return pltpu.CompilerParams(dimension_semantics=sem, vmem_limit_bytes=V7X_VMEM_LIMIT_BYTES)


def _tile(total, target, multiple):
    best = None
    for t in range(multiple, min(total, target) + 1, multiple):
        if total % t == 0:
            best = t
    assert best is not None, (total, target, multiple)
    return best


def _dot(a, b):
    return jnp.dot(a, b, preferred_element_type=F32)


def _dot_nt(a, b):
    return lax.dot_general(a, b, (((1,), (1,)), ((), ())), preferred_element_type=F32)


def _dot_tn(a, b):
    return lax.dot_general(a, b, (((0,), (0,)), ((), ())), preferred_element_type=F32)


def _silu(x):
    return x * jax.nn.sigmoid(x)


def _mod_kernel(c_ref, w_ref, b_ref, o_ref):
    @pl.when(pl.program_id(1) == 0)
    def _():
        o_ref[...] = jnp.broadcast_to(b_ref[...], o_ref.shape)

    o_ref[...] += _dot(_silu(c_ref[...]).astype(BF16), w_ref[...].astype(BF16))


def _modulation(c_rows, w_mod, b_mod):
    depth, d, n = w_mod.shape
    rows = c_rows.shape[0]
    tk = _tile(d, 256, LANES)
    c_chunks = c_rows.reshape(rows, d // tk, tk).swapaxes(0, 1)
    return pl.pallas_call(
        _mod_kernel,
        grid=(depth, d // tk),
        in_specs=[
            pl.BlockSpec((None, rows, tk), lambda l, k: (k, 0, 0)),
            pl.BlockSpec((None, tk, n), lambda l, k: (l, k, 0)),
            pl.BlockSpec((None, 1, n), lambda l, k: (l, 0, 0)),
        ],
        out_specs=pl.BlockSpec((None, rows, n), lambda l, k: (l, 0, 0)),
        out_shape=jax.ShapeDtypeStruct((depth, rows, n), F32),
        compiler_params=_params("arbitrary", "arbitrary"),
        name="modulation",
    )(c_chunks, w_mod, b_mod.reshape(depth, 1, n))


def _log_sigmoid(x):
    return jnp.minimum(x, 0.0) - jnp.log1p(jnp.exp(-jnp.abs(x)))


def _prologue_first_kernel(ctx_ref, lat_ref, g_ref, scale_ref, shift_ref, wg_ref, bgr_ref, bgc_ref,
                           h_ref, gcol_ref, grow_ref, xall_ref, *, n_ctx_tiles):
    @pl.when(pl.program_id(1) < n_ctx_tiles)
    def _():
        xall_ref[...] = ctx_ref[...]

    @pl.when(pl.program_id(1) >= n_ctx_tiles)
    def _():
        xall_ref[...] = lat_ref[...]

    _prologue_kernel(xall_ref, g_ref, scale_ref, shift_ref, wg_ref, bgr_ref, bgc_ref, h_ref, gcol_ref, grow_ref)


def _prologue_kernel(x_ref, g_ref, scale_ref, shift_ref, wg_ref, bgr_ref, bgc_ref, h_ref, gcol_ref, grow_ref):
    x = x_ref[...]
    y = x * lax.rsqrt(jnp.mean(x * x, axis=-1, keepdims=True) + EPS)
    h = (y * g_ref[...]) * (1.0 + scale_ref[...]) + shift_ref[...]
    hb = h.astype(BF16)
    h_ref[...] = hb
    wgb = wg_ref[...].astype(BF16)
    pre_col = _dot_nt(hb, wgb) + bgr_ref[...]
    pre_row = _dot_nt(wgb, hb) + bgc_ref[...]
    gate_c = lax.broadcasted_iota(jnp.int32, pre_col.shape, 1)
    gate_r = lax.broadcasted_iota(jnp.int32, pre_row.shape, 0)
    gcol_ref[...] = jnp.where((gate_c & M_HEADS) != 0, _log_sigmoid(pre_col), pre_col)
    grow_ref[...] = jnp.where((gate_r & M_HEADS) != 0, _log_sigmoid(pre_row), pre_row)


def _prologue(tokens, mod_rows, g_norm, w_t, layer, gate_row, b_gate, ctx_len):
    first = isinstance(tokens, tuple)
    if first:
        ctx, lat = tokens
        bsz, seq, d = lat.shape
        t_all = ctx_len + seq
    else:
        bsz, t_all, d = tokens.shape
    tm = _tile(math.gcd(ctx_len, t_all - ctx_len), 256, LANES)
    n_ctx_tiles = ctx_len // tm
    assert gate_row % N_GATES == 0

    def mod_row(b, t):
        return jnp.where(t < n_ctx_tiles, bsz, b)

    tile_spec = pl.BlockSpec((None, tm, d), lambda b, t: (b, t, 0))
    if first:
        kern = functools.partial(_prologue_first_kernel, n_ctx_tiles=n_ctx_tiles)
        token_specs = [
            pl.BlockSpec((None, tm, d), lambda b, t: (b, jnp.minimum(t, n_ctx_tiles - 1), 0)),
            pl.BlockSpec((None, tm, d), lambda b, t: (b, jnp.maximum(t - n_ctx_tiles, 0), 0)),
        ]
        token_args = [ctx, lat]
    else:
        kern, token_specs, token_args = _prologue_kernel, [tile_spec], [tokens]
    out_specs = [tile_spec,
                 pl.BlockSpec((None, tm, N_GATES), lambda b, t: (b, t, 0)),
                 pl.BlockSpec((None, N_GATES, tm), lambda b, t: (b, 0, t))]
    out_shape = [jax.ShapeDtypeStruct((bsz, t_all, d), BF16),
                 jax.ShapeDtypeStruct((bsz, t_all, N_GATES), F32),
                 jax.ShapeDtypeStruct((bsz, N_GATES, t_all), F32)]
    if first:
        out_specs.append(tile_spec)
        out_shape.append(jax.ShapeDtypeStruct((bsz, t_all, d), F32))
    return pl.pallas_call(
        kern,
        grid=(bsz, t_all // tm),
        in_specs=token_specs + [
            pl.BlockSpec((1, d), lambda b, t: (0, 0)),
            pl.BlockSpec((None, None, 1, d), lambda b, t: (mod_row(b, t), 1, 0, 0)),
            pl.BlockSpec((None, None, 1, d), lambda b, t: (mod_row(b, t), 0, 0, 0)),
            pl.BlockSpec((None, N_GATES, d), lambda b, t: (layer, gate_row // N_GATES, 0)),
            pl.BlockSpec((1, N_GATES), lambda b, t: (0, 0)),
            pl.BlockSpec((N_GATES, 1), lambda b, t: (0, 0)),
        ],
        out_specs=out_specs,
        out_shape=out_shape,
        compiler_params=_params("arbitrary", "arbitrary"),
        name="prologue",
    )(*token_args, g_norm.reshape(1, d), mod_rows, mod_rows, w_t,
      b_gate.reshape(1, N_GATES), b_gate.reshape(N_GATES, 1))


def _rope_tables(seq, ctx_len):
    axis_dim = HEAD_DIM // 2
    rows = seq // GRID_W
    row_ids = jnp.repeat(jnp.arange(rows), GRID_W).astype(F32)
    col_ids = jnp.tile(jnp.arange(GRID_W), rows).astype(F32)
    inv = ROPE_THETA ** (-jnp.arange(0, axis_dim, 2, dtype=F32) / axis_dim)
    ang_r, ang_c = row_ids[:, None] * inv, col_ids[:, None] * inv
    cos_t = jnp.concatenate([jnp.cos(ang_r)] * 2 + [jnp.cos(ang_c)] * 2, axis=-1)
    sin_t = jnp.concatenate([-jnp.sin(ang_r), jnp.sin(ang_r), -jnp.sin(ang_c), jnp.sin(ang_c)], axis=-1)
    cos_t = jnp.concatenate([jnp.ones((ctx_len, HEAD_DIM), F32), cos_t], axis=0)
    sin_t = jnp.concatenate([jnp.zeros((ctx_len, HEAD_DIM), F32), sin_t], axis=0)
    return cos_t, sin_t


def _in_proj_kernel(a_ref, w_ref, cos_ref, sin_ref, gq_ref, gk_ref, pf_ref, qkv_ref, wb_ref, *, q_lo, k_lo, v_lo, z_lo):
    j = pl.program_id(0)

    @pl.when(pl.program_id(1) == 0)
    def _():
        wb_ref[...] = w_ref[...].astype(BF16)

    y = _dot_nt(a_ref[...], wb_ref[...])

    def norm_rope(g_ref, post_scale):
        cos_t, sin_t = cos_ref[...], sin_ref[...]
        quarter = HEAD_DIM // 4
        lane = lax.broadcasted_iota(jnp.int32, cos_t.shape, 1)
        first_half = (lane % (2 * quarter)) < quarter
        for hd in range(y.shape[-1] // HEAD_DIM):
            cols = slice(hd * HEAD_DIM, (hd + 1) * HEAD_DIM)
            x = y[:, cols]
            xn = x * lax.rsqrt(jnp.mean(x * x, axis=-1, keepdims=True) + EPS) * g_ref[...]
            swapped = jnp.where(first_half, pltpu.roll(xn, HEAD_DIM - quarter, axis=1),
                                pltpu.roll(xn, quarter, axis=1))
            qkv_ref[:, cols] = ((xn * cos_t + swapped * sin_t) * post_scale).astype(BF16)

    @pl.when(jnp.logical_or(j < q_lo, j >= z_lo))
    def _():
        pf_ref[...] = y

    @pl.when(jnp.logical_and(j >= q_lo, j < k_lo))
    def _():
        norm_rope(gq_ref, HEAD_DIM ** -0.5 * LOG2_E)

    @pl.when(jnp.logical_and(j >= k_lo, j < v_lo))
    def _():
        norm_rope(gk_ref, 1.0)

    @pl.when(jnp.logical_and(j >= v_lo, j < z_lo))
    def _():
        qkv_ref[...] = y.astype(BF16)


def _in_projection(h3, w_t, layer, cos_t, sin_t, g_q, g_k, mw, aw):
    bsz, t_all, d = h3.shape
    m = bsz * t_all
    kvw = aw // GQA_GROUP
    tn = _tile(kvw, 512, LANES)
    tm = _tile(t_all, 1152, 16)
    tiles_per_batch = t_all // tm
    n_i = m // tm
    q_lo = 4 * mw // tn
    k_lo = q_lo + aw // tn
    v_lo = k_lo + kvw // tn
    z_lo = v_lo + kvw // tn
    n_j = z_lo + aw // tn
    n_bf = z_lo - q_lo

    def w_row(j, i):
        return pl.multiple_of(j * tn + jnp.where(j >= q_lo, N_GATES, 0), N_GATES)

    def in_f32(j):
        return jnp.logical_or(j < q_lo, j >= z_lo)

    def pf_idx(j, i):
        return (jnp.where(in_f32(j), i, n_i - 1), jnp.where(j < q_lo, j, jnp.where(j < z_lo, q_lo - 1, j - n_bf)))

    def qkv_idx(j, i):
        return (jnp.where(j < q_lo, 0, jnp.where(j < z_lo, i, n_i - 1)), jnp.clip(j - q_lo, 0, n_bf - 1))

    kern = functools.partial(_in_proj_kernel, q_lo=q_lo, k_lo=k_lo, v_lo=v_lo, z_lo=z_lo)
    pf, qkv = pl.pallas_call(
        kern,
        grid=(n_j, n_i),
        in_specs=[
            pl.BlockSpec((tm, d), lambda j, i: (i, 0)),
            pl.BlockSpec((None, pl.Element(tn), pl.Element(d)), lambda j, i: (layer, w_row(j, i), 0)),
            pl.BlockSpec((tm, HEAD_DIM), lambda j, i: (i % tiles_per_batch, 0)),
            pl.BlockSpec((tm, HEAD_DIM), lambda j, i: (i % tiles_per_batch, 0)),
            pl.BlockSpec((1, HEAD_DIM), lambda j, i: (0, 0)),
            pl.BlockSpec((1, HEAD_DIM), lambda j, i: (0, 0)),
        ],
        out_specs=[
            pl.BlockSpec((tm, tn), pf_idx),
            pl.BlockSpec((tm, tn), qkv_idx),
        ],
        out_shape=[
            jax.ShapeDtypeStruct((m, 4 * mw + aw), F32),
            jax.ShapeDtypeStruct((m, aw + 2 * kvw), BF16),
        ],
        scratch_shapes=[pltpu.VMEM((tn, d), BF16)],
        compiler_params=_params("arbitrary", "arbitrary"),
        name="in_projection",
    )(h3.reshape(m, d), w_t, cos_t, sin_t, g_q.reshape(1, HEAD_DIM), g_k.reshape(1, HEAD_DIM))
    return pf.reshape(bsz, t_all, -1), qkv.reshape(bsz, t_all, -1)


def _mlstm_chunk(q, k, v, i_col, f_col, i_row, f_row, ct, n, m, reverse):
    L = q.shape[0]
    t_idx = lax.broadcasted_iota(jnp.int32, (L, L), 0)
    s_idx = lax.broadcasted_iota(jnp.int32, (L, L), 1)
    if reverse:
        seen = s_idx >= t_idx
        seen_t = t_idx >= s_idx
        last = 0
    else:
        seen = s_idx <= t_idx
        seen_t = t_idx <= s_idx
        last = L - 1
    b_col = jnp.sum(jnp.where(seen, f_row, 0.0), axis=1, keepdims=True)
    b_row = jnp.sum(jnp.where(seen_t, f_col, 0.0), axis=0, keepdims=True)
    logw = jnp.where(seen, b_col - b_row + i_row, -jnp.inf)
    m_state = b_col + m
    m_t = jnp.maximum(m_state, jnp.max(logw, axis=1, keepdims=True))
    w_state = jnp.exp(m_state - m_t)
    qb, kb, vb = q.astype(BF16), k.astype(BF16), v.astype(BF16)
    s = _dot_nt(qb, kb) * jnp.exp(logw - m_t)
    num = w_state * _dot(qb, ct.astype(BF16)) + _dot(s.astype(BF16), vb)
    den = (w_state * jnp.sum(q * n, axis=1, keepdims=True)
           + jnp.sum(s, axis=1, keepdims=True))
    h = num / jnp.maximum(jnp.abs(den), jnp.exp(-m_t))
    m_new = m_t[last:last + 1, :]
    b_last = b_col[last:last + 1, :]
    w_s = jnp.exp(b_last - b_col + i_col - m_new)
    a_state = jnp.exp(b_last + m - m_new)
    kw = k * w_s
    ct_new = a_state * ct + _dot_tn(kw.astype(BF16), vb)
    n_new = a_state * n + jnp.sum(kw, axis=0, keepdims=True)
    return h, ct_new, n_new, m_new


def _mlstm_heads(q_ref, k_ref, v_ref, gcol_ref, grow_ref, ct_ref, n_ref, m_ref, reverse, emit):
    dk = q_ref.shape[-1] // M_HEADS
    dv = v_ref.shape[-1] // M_HEADS
    g0 = 2 * M_HEADS if reverse else 0

    @pl.when(pl.program_id(1) == 0)
    def _():
        ct_ref[...] = jnp.zeros_like(ct_ref)
        n_ref[...] = jnp.zeros_like(n_ref)
        m_ref[...] = jnp.zeros_like(m_ref)

    for hd in range(M_HEADS):
        q = q_ref[:, hd * dk:(hd + 1) * dk] * (dk ** -0.5)
        k = k_ref[:, hd * dk:(hd + 1) * dk]
        v = v_ref[:, hd * dv:(hd + 1) * dv]
        i_col = gcol_ref[:, g0 + hd:g0 + hd + 1]
        f_col = gcol_ref[:, g0 + M_HEADS + hd:g0 + M_HEADS + hd + 1]
        i_row = grow_ref[g0 + hd:g0 + hd + 1, :]
        f_row = grow_ref[g0 + M_HEADS + hd:g0 + M_HEADS + hd + 1, :]
        h, ct_new, n_new, m_new = _mlstm_chunk(
            q, k, v, i_col, f_col, i_row, f_row, ct_ref[hd], n_ref[hd], m_ref[hd], reverse)
        ct_ref[hd] = ct_new
        n_ref[hd] = n_new
        m_ref[hd] = m_new
        emit(hd, h)


def _mlstm_fwd_kernel(q_ref, k_ref, v_ref, gcol_ref, grow_ref, h_ref, ct_ref, n_ref, m_ref):
    dv = v_ref.shape[-1] // M_HEADS

    def emit(hd, h):
        h_ref[:, hd * dv:(hd + 1) * dv] = h

    _mlstm_heads(q_ref, k_ref, v_ref, gcol_ref, grow_ref, ct_ref, n_ref, m_ref, False, emit)


def _mlstm_bwd_kernel(q_ref, k_ref, v_ref, gcol_ref, grow_ref, hf_ref, o_ref, z_ref, gm_ref,
                      out_ref, ct_ref, n_ref, m_ref):
    dv = v_ref.shape[-1] // M_HEADS

    def emit(hd, h):
        cols = slice(hd * dv, (hd + 1) * dv)
        hs = hf_ref[:, cols] + h
        hn = hs * lax.rsqrt(jnp.mean(hs * hs, axis=-1, keepdims=True) + EPS) * gm_ref[:, cols]
        gated = hn * jax.nn.sigmoid(o_ref[:, cols]) * _silu(z_ref[:, cols])
        out_ref[:, cols] = gated.astype(out_ref.dtype)

    _mlstm_heads(q_ref, k_ref, v_ref, gcol_ref, grow_ref, ct_ref, n_ref, m_ref, True, emit)


def _mlstm(pf, gcol, grow, g_mlstm, ctx_len):
    bsz, t_all, _ = pf.shape
    mw = g_mlstm.shape[-1]
    qw = mw // 2
    nc = t_all // CHUNK
    n_ctx = ctx_len // CHUNK
    dk, dv = qw // M_HEADS, mw // M_HEADS

    def rev_chunk(c):
        return jnp.where(c < n_ctx, n_ctx - 1 - c, nc - 1 - (c - n_ctx))

    def specs(chunk_of):
        return [
            pl.BlockSpec((None, CHUNK, qw), lambda b, c: (b, chunk_of(c), 0)),
            pl.BlockSpec((None, CHUNK, qw), lambda b, c: (b, chunk_of(c), 1)),
            pl.BlockSpec((None, CHUNK, mw), lambda b, c: (b, chunk_of(c), 1)),
            pl.BlockSpec((None, CHUNK, N_GATES), lambda b, c: (b, chunk_of(c), 0)),
            pl.BlockSpec((None, N_GATES, CHUNK), lambda b, c: (b, 0, chunk_of(c))),
        ]

    scratch = [pltpu.VMEM((M_HEADS, dk, dv), F32), pltpu.VMEM((M_HEADS, 1, dk), F32),
               pltpu.VMEM((M_HEADS, 1, 1), F32)]
    h_fwd = pl.pallas_call(
        _mlstm_fwd_kernel,
        grid=(bsz, nc),
        in_specs=specs(lambda c: c),
        out_specs=pl.BlockSpec((None, CHUNK, mw), lambda b, c: (b, c, 0)),
        out_shape=jax.ShapeDtypeStruct((bsz, t_all, mw), F32),
        scratch_shapes=scratch,
        compiler_params=_params("arbitrary", "arbitrary"),
        name="mlstm_fwd",
    )(pf, pf, pf, gcol, grow)
    return pl.pallas_call(
        _mlstm_bwd_kernel,
        grid=(bsz, nc),
        in_specs=specs(rev_chunk) + [
            pl.BlockSpec((None, CHUNK, mw), lambda b, c: (b, rev_chunk(c), 0)),
            pl.BlockSpec((None, CHUNK, mw), lambda b, c: (b, rev_chunk(c), 2)),
            pl.BlockSpec((None, CHUNK, mw), lambda b, c: (b, rev_chunk(c), 3)),
            pl.BlockSpec((1, mw), lambda b, c: (0, 0)),
        ],
        out_specs=pl.BlockSpec((None, CHUNK, mw), lambda b, c: (b, rev_chunk(c), 0)),
        out_shape=jax.ShapeDtypeStruct((bsz, t_all, mw), BF16),
        scratch_shapes=scratch,
        compiler_params=_params("arbitrary", "arbitrary"),
        name="mlstm_bwd",
    )(pf, pf, pf, gcol, grow, h_fwd, pf, pf, g_mlstm.reshape(1, mw))


def _attn_kernel(q_ref, k_ref, v_ref, z_ref, o_ref, vx_ref, s_ref, p_ref, *, ctx_len, key_chunk, update_ctx):
    tq = q_ref.shape[0]
    t_all = k_ref.shape[0]
    n_split = 2
    heads_per_split = GQA_GROUP // n_split
    rows_per_split = heads_per_split * tq

    @pl.when(pl.program_id(2) == 0)
    def _():
        vx_ref[:, :HEAD_DIM] = v_ref[...]
        vx_ref[:, HEAD_DIM:] = jnp.ones((t_all, HEAD_DIM), BF16)

    def attend(n_keys):
        n_tiles = n_keys // LANES
        for sp in range(n_split):
            q = jnp.concatenate([q_ref[:, g * HEAD_DIM:(g + 1) * HEAD_DIM]
                                 for g in range(sp * heads_per_split, (sp + 1) * heads_per_split)], axis=0)
            rows = slice(sp * rows_per_split, (sp + 1) * rows_per_split)
            for lo in range(0, n_keys, key_chunk):
                hi = min(lo + key_chunk, n_keys)
                s_ref[rows, lo:hi] = _dot_nt(q, k_ref[lo:hi, :])
        for sp in range(n_split):
            for rb in range(rows_per_split // SOFTMAX_ROWS):
                r0 = sp * rows_per_split + rb * SOFTMAX_ROWS
                rows = slice(r0, r0 + SOFTMAX_ROWS)
                m_lanes = s_ref[rows, 0:LANES]
                for t in range(1, n_tiles):
                    m_lanes = jnp.maximum(m_lanes, s_ref[rows, t * LANES:(t + 1) * LANES])
                m_rows = jnp.broadcast_to(jnp.max(m_lanes, axis=-1, keepdims=True), (SOFTMAX_ROWS, LANES))
                for t in range(n_tiles):
                    cols = slice(t * LANES, (t + 1) * LANES)
                    p_ref[rows, cols] = jnp.exp2(s_ref[rows, cols] - m_rows).astype(BF16)
            rows = slice(sp * rows_per_split, (sp + 1) * rows_per_split)
            ov = _dot(p_ref[rows, 0:n_keys], vx_ref[0:n_keys, :])
            o = ov[:, :HEAD_DIM] / ov[:, HEAD_DIM:HEAD_DIM + 1]
            for gl in range(heads_per_split):
                g = sp * heads_per_split + gl
                cols = slice(g * HEAD_DIM, (g + 1) * HEAD_DIM)
                o_ref[:, cols] = (o[gl * tq:(gl + 1) * tq] * _silu(z_ref[:, cols])).astype(o_ref.dtype)

    @pl.when(pl.program_id(2) == 0)
    def _():
        if update_ctx:
            attend(ctx_len)
        else:
            o_ref[...] = jnp.zeros_like(o_ref)

    @pl.when(pl.program_id(2) > 0)
    def _():
        attend(t_all)


def _attention(qkv, pf, mw, aw, ctx_len, update_ctx):
    bsz, t_all, _ = qkv.shape
    kvw = aw // GQA_GROUP
    kv_heads = kvw // HEAD_DIM
    gw = GQA_GROUP * HEAD_DIM
    tq = ctx_len
    assert (t_all - ctx_len) % tq == 0 and tq % SOFTMAX_ROWS == 0
    k_blk = aw // HEAD_DIM
    v_blk = (aw + kvw) // HEAD_DIM
    z_blk = 4 * mw // gw
    kern = functools.partial(_attn_kernel, ctx_len=ctx_len, key_chunk=512, update_ctx=update_ctx)
    return pl.pallas_call(
        kern,
        grid=(bsz, kv_heads, t_all // tq),
        in_specs=[
            pl.BlockSpec((None, tq, gw), lambda b, h, i: (b, i, h)),
            pl.BlockSpec((None, t_all, HEAD_DIM), lambda b, h, i: (b, 0, k_blk + h)),
            pl.BlockSpec((None, t_all, HEAD_DIM), lambda b, h, i: (b, 0, v_blk + h)),
            pl.BlockSpec((None, tq, gw), lambda b, h, i: (b, i, z_blk + h)),
        ],
        out_specs=pl.BlockSpec((None, tq, gw), lambda b, h, i: (b, i, h)),
        out_shape=jax.ShapeDtypeStruct((bsz, t_all, aw), BF16),
        scratch_shapes=[pltpu.VMEM((t_all, 2 * HEAD_DIM), BF16),
                        pltpu.VMEM((GQA_GROUP * tq, t_all), F32),
                        pltpu.VMEM((GQA_GROUP * tq, t_all), BF16)],
        compiler_params=_params("arbitrary", "arbitrary", "arbitrary"),
        name="attention",
    )(qkv, qkv, qkv, pf)


def _out_proj_kernel(am_ref, aa_ref, wm_ref, wa_ref, x_ref, gl_ref, gc_ref, o_ref, *, ctx_len, tiles_per_batch):
    tm = x_ref.shape[0]
    y = _dot(am_ref[...], wm_ref[...].astype(BF16)) + _dot(aa_ref[...], wa_ref[...].astype(BF16))
    row0 = (pl.program_id(0) % tiles_per_batch) * tm
    row = row0 + lax.broadcasted_iota(jnp.int32, y.shape, 0)
    gate = jnp.where(row < ctx_len, gc_ref[...], gl_ref[...])
    o_ref[...] = x_ref[...] + gate * y


def _out_projection(m_out, a_out, w_out, layer, x_all, mod_rows, ctx_len):
    bsz, t_all, d = x_all.shape
    mw = m_out.shape[-1]
    aw = a_out.shape[-1]
    assert mw == aw
    tm = _tile(t_all, 1152, 16)
    tpb = t_all // tm
    tn = _tile(d, 512, LANES)
    m = bsz * t_all
    kern = functools.partial(_out_proj_kernel, ctx_len=ctx_len, tiles_per_batch=tpb)
    out = pl.pallas_call(
        kern,
        grid=(m // tm, d // tn),
        in_specs=[
            pl.BlockSpec((tm, mw), lambda i, j: (i, 0)),
            pl.BlockSpec((tm, aw), lambda i, j: (i, 0)),
            pl.BlockSpec((None, mw, tn), lambda i, j: (layer, 0, j)),
            pl.BlockSpec((None, aw, tn), lambda i, j: (layer, 1, j)),
            pl.BlockSpec((tm, tn), lambda i, j: (i, j)),
            pl.BlockSpec((None, None, 1, tn), lambda i, j: (i // tpb, 2, 0, j)),
            pl.BlockSpec((None, None, 1, tn), lambda i, j: (bsz, 2, 0, j)),
        ],
        out_specs=pl.BlockSpec((tm, tn), lambda i, j: (i, j)),
        out_shape=jax.ShapeDtypeStruct((m, d), F32),
        compiler_params=_params("arbitrary", "arbitrary"),
        name="out_projection",
    )(m_out.reshape(m, mw), a_out.reshape(m, aw), w_out, w_out, x_all.reshape(m, d), mod_rows, mod_rows)
    return out.reshape(bsz, t_all, d)


def _final_norm_kernel(x_ref, g_ref, o_ref):
    x = x_ref[...]
    o_ref[...] = x * lax.rsqrt(jnp.mean(x * x, axis=-1, keepdims=True) + EPS) * g_ref[...]


def _final_norm(x_all, g_final, ctx_len):
    bsz, t_all, d = x_all.shape
    seq = t_all - ctx_len
    tm = _tile(math.gcd(ctx_len, seq), 256, 8)
    off = ctx_len // tm
    return pl.pallas_call(
        _final_norm_kernel,
        grid=(bsz, seq // tm),
        in_specs=[
            pl.BlockSpec((None, tm, d), lambda b, t: (b, t + off, 0)),
            pl.BlockSpec((1, d), lambda b, t: (0, 0)),
        ],
        out_specs=pl.BlockSpec((None, tm, d), lambda b, t: (b, t, 0)),
        out_shape=jax.ShapeDtypeStruct((bsz, seq, d), F32),
        compiler_params=_params("arbitrary", "arbitrary"),
        name="final_norm",
    )(x_all, g_final.reshape(1, d))


def kernel(x, c, ctx, c_ctx, w_mod, b_mod, g_norm, w_in, b_gate, g_mlstm, g_q, g_k, w_out, g_final):
    bsz, seq, d = x.shape
    ctx_len = ctx.shape[1]
    depth = w_mod.shape[0]
    mw = g_mlstm.shape[-1]
    aw = w_out.shape[1] - mw
    gate_col = 4 * mw
    assert mw == aw and w_in.shape[-1] == gate_col + N_GATES + aw * 5 // 2
    assert ctx_len % CHUNK == 0 and seq % CHUNK == 0 and seq % GRID_W == 0

    w_t = jnp.swapaxes(w_in, 1, 2)
    cos_t, sin_t = _rope_tables(seq, ctx_len)

    c_rows = jnp.concatenate([c, c_ctx[None, :]], axis=0)
    n_rows = bsz + 1
    c_rows = jnp.pad(c_rows, ((0, -n_rows % 8), (0, 0)))
    mod = _modulation(c_rows, w_mod, b_mod)
    mod = mod[:, :n_rows].reshape(depth, n_rows, 3, 1, d)

    x_all = None
    for layer in range(depth):
        update_ctx = layer < depth - 1
        prologue_args = (mod[layer], g_norm[layer], w_t, layer, gate_col, b_gate[layer], ctx_len)
        if layer == 0:
            h, gcol, grow, x_all = _prologue((ctx, x), *prologue_args)
        else:
            h, gcol, grow = _prologue(x_all, *prologue_args)
        pf, qkv = _in_projection(h, w_t, layer, cos_t, sin_t, g_q[layer], g_k[layer], mw, aw)
        m_out = _mlstm(pf, gcol, grow, g_mlstm[layer], ctx_len)
        a_out = _attention(qkv, pf, mw, aw, ctx_len, update_ctx)
        x_all = _out_projection(m_out, a_out, w_out, layer, x_all, mod[layer], ctx_len)
    return _final_norm(x_all, g_final, ctx_len)
```

```python
import functools
import math

import jax
import jax.numpy as jnp
from jax import lax
from jax.experimental import pallas as pl
from jax.experimental.pallas import tpu as pltpu

CHUNK = 128
M_HEADS = 4
HEAD_DIM = 128
GQA_GROUP = 4
GRID_W = 64
ROPE_THETA = 10000.0
EPS = 1e-6
N_GATES = 4 * M_HEADS
LOG2_E = 1.4426950408889634
SOFTMAX_ROWS = 128

LANES = 128
V7X_VMEM_LIMIT_BYTES = 56 * 1024 * 1024

F32 = jnp.float32
BF16 = jnp.bfloat16


def _params(*sem):
    return pltpu.CompilerParams(dimension_semantics=sem, vmem_limit_bytes=V7X_VMEM_LIMIT_BYTES)


def _tile(total, target, multiple):
    best = None
    for t in range(multiple, min(total, target) + 1, multiple):
        if total % t == 0:
            best = t
    assert best is not None, (total, target, multiple)
    return best


def _dot(a, b):
    return jnp.dot(a, b, preferred_element_type=F32)


def _dot_nt(a, b):
    return lax.dot_general(a, b, (((1,), (1,)), ((), ())), preferred_element_type=F32)


def _dot_tn(a, b):
    return lax.dot_general(a, b, (((0,), (0,)), ((), ())), preferred_element_type=F32)


def _silu(x):
    return x * jax.nn.sigmoid(x)


def _mod_kernel(c_ref, w_ref, b_ref, o_ref):
    @pl.when(pl.program_id(1) == 0)
    def _():
        o_ref[...] = jnp.broadcast_to(b_ref[...], o_ref.shape)

    o_ref[...] += _dot(_silu(c_ref[...]).astype(BF16), w_ref[...].astype(BF16))


def _modulation(c_rows, w_mod, b_mod):
    depth, d, n = w_mod.shape
    rows = c_rows.shape[0]
    tk = _tile(d, 256, LANES)
    c_chunks = c_rows.reshape(rows, d // tk, tk).swapaxes(0, 1)
    return pl.pallas_call(
        _mod_kernel,
        grid=(depth, d // tk),
        in_specs=[
            pl.BlockSpec((None, rows, tk), lambda l, k: (k, 0, 0)),
            pl.BlockSpec((None, tk, n), lambda l, k: (l, k, 0)),
            pl.BlockSpec((None, 1, n), lambda l, k: (l, 0, 0)),
        ],
        out_specs=pl.BlockSpec((None, rows, n), lambda l, k: (l, 0, 0)),
        out_shape=jax.ShapeDtypeStruct((depth, rows, n), F32),
        compiler_params=_params("arbitrary", "arbitrary"),
        name="modulation",
    )(c_chunks, w_mod, b_mod.reshape(depth, 1, n))


def _log_sigmoid(x):
    return jnp.minimum(x, 0.0) - jnp.log1p(jnp.exp(-jnp.abs(x)))


def _prologue_first_kernel(ctx_ref, lat_ref, g_ref, scale_ref, shift_ref, wg_ref, bgr_ref, bgc_ref,
                           h_ref, gcol_ref, grow_ref, xall_ref, *, n_ctx_tiles):
    @pl.when(pl.program_id(1) < n_ctx_tiles)
    def _():
        xall_ref[...] = ctx_ref[...]

    @pl.when(pl.program_id(1) >= n_ctx_tiles)
    def _():
        xall_ref[...] = lat_ref[...]

    _prologue_kernel(xall_ref, g_ref, scale_ref, shift_ref, wg_ref, bgr_ref, bgc_ref, h_ref, gcol_ref, grow_ref)


def _prologue_kernel(x_ref, g_ref, scale_ref, shift_ref, wg_ref, bgr_ref, bgc_ref, h_ref, gcol_ref, grow_ref):
    x = x_ref[...]
    y = x * lax.rsqrt(jnp.mean(x * x, axis=-1, keepdims=True) + EPS)
    h = (y * g_ref[...]) * (1.0 + scale_ref[...]) + shift_ref[...]
    hb = h.astype(BF16)
    h_ref[...] = hb
    _scan_gate_terms(hb, wg_ref[...], bgr_ref[...], bgc_ref[...], gcol_ref, grow_ref)


def _split3(x):
    hi = x.astype(BF16)
    rest = x - hi.astype(F32)
    mid = rest.astype(BF16)
    return hi, mid, (rest - mid.astype(F32)).astype(BF16)


def _scan_gate_terms(hb, wg, bg_row, bg_col, gcol_ref, grow_ref):
    H = M_HEADS
    tm = hb.shape[0]
    wi = jnp.concatenate([wg[0:H], wg[2 * H:3 * H]], axis=0).astype(BF16)
    wf = jnp.concatenate([wg[H:2 * H], wg[3 * H:4 * H]], axis=0).astype(BF16)
    bi_r = jnp.concatenate([bg_row[:, 0:H], bg_row[:, 2 * H:3 * H]], axis=1)
    bf_r = jnp.concatenate([bg_row[:, H:2 * H], bg_row[:, 3 * H:4 * H]], axis=1)
    bi_c = jnp.concatenate([bg_col[0:H], bg_col[2 * H:3 * H]], axis=0)
    bf_c = jnp.concatenate([bg_col[H:2 * H], bg_col[3 * H:4 * H]], axis=0)
    f_col = _log_sigmoid(_dot_nt(hb, wf) + bf_r)
    i_row = _dot_nt(wi, hb) + bi_c
    f_row = _log_sigmoid(_dot_nt(wf, hb) + bf_c)
    t_idx = lax.broadcasted_iota(jnp.int32, (tm, tm), 0)
    s_idx = lax.broadcasted_iota(jnp.int32, (tm, tm), 1)
    same_chunk = (t_idx // CHUNK) == (s_idx // CHUNK)
    at_or_before = jnp.logical_and(same_chunk, s_idx <= t_idx)
    at_or_after = jnp.logical_and(same_chunk, s_idx >= t_idx)
    lower = jnp.where(at_or_before, 1.0, 0.0).astype(BF16)
    upper = jnp.where(at_or_after, 1.0, 0.0).astype(BF16)

    def cumulate(parts, mat, left):
        return sum(_dot(mat, p) if left else _dot(p, mat) for p in parts)

    f_col_parts, f_row_parts = _split3(f_col), _split3(f_row)
    bwd_c = lax.broadcasted_iota(jnp.int32, (tm, 2 * H), 1) >= H
    bwd_r = lax.broadcasted_iota(jnp.int32, (2 * H, tm), 0) >= H
    b_col = jnp.where(bwd_c, cumulate(f_col_parts, upper, True), cumulate(f_col_parts, lower, True))
    b_row = jnp.where(bwd_r, cumulate(f_row_parts, lower, False), cumulate(f_row_parts, upper, False))
    a_row = i_row - b_row
    cm_cols = []
    for st in range(2 * H):
        seen = at_or_after if st >= H else at_or_before
        cm_cols.append(jnp.max(jnp.where(seen, a_row[st:st + 1, :], -jnp.inf), axis=1, keepdims=True))
    gcol_ref[...] = jnp.concatenate(cm_cols + [b_col], axis=1)
    grow_ref[...] = a_row


def _prologue(tokens, mod_rows, g_norm, w_t, layer, gate_row, b_gate, ctx_len):
    first = isinstance(tokens, tuple)
    if first:
        ctx, lat = tokens
        bsz, seq, d = lat.shape
        t_all = ctx_len + seq
    else:
        bsz, t_all, d = tokens.shape
    tm = _tile(math.gcd(ctx_len, t_all - ctx_len), 256, LANES)
    n_ctx_tiles = ctx_len // tm
    assert gate_row % N_GATES == 0

    def mod_row(b, t):
        return jnp.where(t < n_ctx_tiles, bsz, b)

    tile_spec = pl.BlockSpec((None, tm, d), lambda b, t: (b, t, 0))
    if first:
        kern = functools.partial(_prologue_first_kernel, n_ctx_tiles=n_ctx_tiles)
        token_specs = [
            pl.BlockSpec((None, tm, d), lambda b, t: (b, jnp.minimum(t, n_ctx_tiles - 1), 0)),
            pl.BlockSpec((None, tm, d), lambda b, t: (b, jnp.maximum(t - n_ctx_tiles, 0), 0)),
        ]
        token_args = [ctx, lat]
    else:
        kern, token_specs, token_args = _prologue_kernel, [tile_spec], [tokens]
    out_specs = [tile_spec,
                 pl.BlockSpec((None, tm, N_GATES), lambda b, t: (b, t, 0)),
                 pl.BlockSpec((None, 2 * M_HEADS, tm), lambda b, t: (b, 0, t))]
    out_shape = [jax.ShapeDtypeStruct((bsz, t_all, d), BF16),
                 jax.ShapeDtypeStruct((bsz, t_all, N_GATES), F32),
                 jax.ShapeDtypeStruct((bsz, 2 * M_HEADS, t_all), F32)]
    if first:
        out_specs.append(tile_spec)
        out_shape.append(jax.ShapeDtypeStruct((bsz, t_all, d), F32))
    return pl.pallas_call(
        kern,
        grid=(bsz, t_all // tm),
        in_specs=token_specs + [
            pl.BlockSpec((1, d), lambda b, t: (0, 0)),
            pl.BlockSpec((None, None, 1, d), lambda b, t: (mod_row(b, t), 1, 0, 0)),
            pl.BlockSpec((None, None, 1, d), lambda b, t: (mod_row(b, t), 0, 0, 0)),
            pl.BlockSpec((None, N_GATES, d), lambda b, t: (layer, gate_row // N_GATES, 0)),
            pl.BlockSpec((1, N_GATES), lambda b, t: (0, 0)),
            pl.BlockSpec((N_GATES, 1), lambda b, t: (0, 0)),
        ],
        out_specs=out_specs,
        out_shape=out_shape,
        compiler_params=_params("arbitrary", "arbitrary"),
        name="prologue",
    )(*token_args, g_norm.reshape(1, d), mod_rows, mod_rows, w_t,
      b_gate.reshape(1, N_GATES), b_gate.reshape(N_GATES, 1))


def _rope_tables(seq, ctx_len):
    axis_dim = HEAD_DIM // 2
    rows = seq // GRID_W
    row_ids = jnp.repeat(jnp.arange(rows), GRID_W).astype(F32)
    col_ids = jnp.tile(jnp.arange(GRID_W), rows).astype(F32)
    inv = ROPE_THETA ** (-jnp.arange(0, axis_dim, 2, dtype=F32) / axis_dim)
    ang_r, ang_c = row_ids[:, None] * inv, col_ids[:, None] * inv
    cos_t = jnp.concatenate([jnp.cos(ang_r)] * 2 + [jnp.cos(ang_c)] * 2, axis=-1)
    sin_t = jnp.concatenate([-jnp.sin(ang_r), jnp.sin(ang_r), -jnp.sin(ang_c), jnp.sin(ang_c)], axis=-1)
    cos_t = jnp.concatenate([jnp.ones((ctx_len, HEAD_DIM), F32), cos_t], axis=0)
    sin_t = jnp.concatenate([jnp.zeros((ctx_len, HEAD_DIM), F32), sin_t], axis=0)
    return cos_t, sin_t


def _skip_mlstm_k(j, mq_hi, mv_lo):
    return j + jnp.where(j >= mq_hi, mv_lo - mq_hi, 0)


def _in_proj_kernel(a_ref, w_ref, cos_ref, sin_ref, gq_ref, gk_ref, pb_ref, pf_ref, wb_ref, *,
                    mq_hi, mq_scale, mv_lo, oz_lo, q_lo, k_lo, v_lo, z_lo):
    j = _skip_mlstm_k(pl.program_id(0), mq_hi, mv_lo)

    @pl.when(pl.program_id(1) == 0)
    def _():
        wb_ref[...] = w_ref[...].astype(BF16)

    y = _dot_nt(a_ref[...], wb_ref[...])

    def norm_rope(g_ref, post_scale):
        cos_t, sin_t = cos_ref[...], sin_ref[...]
        quarter = HEAD_DIM // 4
        lane = lax.broadcasted_iota(jnp.int32, cos_t.shape, 1)
        first_half = (lane % (2 * quarter)) < quarter
        for hd in range(y.shape[-1] // HEAD_DIM):
            cols = slice(hd * HEAD_DIM, (hd + 1) * HEAD_DIM)
            x = y[:, cols]
            xn = x * lax.rsqrt(jnp.mean(x * x, axis=-1, keepdims=True) + EPS) * g_ref[...]
            swapped = jnp.where(first_half, pltpu.roll(xn, HEAD_DIM - quarter, axis=1),
                                pltpu.roll(xn, quarter, axis=1))
            pb_ref[:, cols] = ((xn * cos_t + swapped * sin_t) * post_scale).astype(BF16)

    @pl.when(jnp.logical_or(jnp.logical_and(j >= oz_lo, j < q_lo), j >= z_lo))
    def _():
        pf_ref[...] = y

    @pl.when(jnp.logical_or(j < oz_lo, jnp.logical_and(j >= v_lo, j < z_lo)))
    def _():
        pb_ref[...] = (y * jnp.where(j < mq_hi, mq_scale, 1.0)).astype(BF16)

    @pl.when(jnp.logical_and(j >= q_lo, j < k_lo))
    def _():
        norm_rope(gq_ref, HEAD_DIM ** -0.5 * LOG2_E)

    @pl.when(jnp.logical_and(j >= k_lo, j < v_lo))
    def _():
        norm_rope(gk_ref, 1.0)


def _held_block_index(j, i, n_i, written):
    (lo0, _, shift0) = written[0]
    row, col = jnp.where(j < lo0, 0, i), jnp.where(j < lo0, lo0 - shift0, j - shift0)
    for k, (lo, hi, shift) in enumerate(written):
        nxt = written[k + 1][0] if k + 1 < len(written) else None
        held = j >= hi if nxt is None else jnp.logical_and(j >= hi, j < nxt)
        row = jnp.where(held, n_i - 1, row)
        col = jnp.where(held, hi - 1 - shift, col)
        if k > 0:
            col = jnp.where(jnp.logical_and(j >= lo, j < hi), j - shift, col)
    return row, col


def _in_projection(h3, w_t, layer, cos_t, sin_t, g_q, g_k, mw, aw):
    bsz, t_all, d = h3.shape
    m = bsz * t_all
    kvw = aw // GQA_GROUP
    qw = mw // 2
    tn = _tile(kvw, 512, LANES)
    tm = _tile(t_all, 1152, 16)
    tiles_per_batch = t_all // tm
    n_i = m // tm
    mq_hi = qw // tn
    mv_lo = 2 * mq_hi
    oz_lo = 2 * mw // tn
    q_lo = 4 * mw // tn
    k_lo = q_lo + aw // tn
    v_lo = k_lo + kvw // tn
    z_lo = v_lo + kvw // tn
    n_j = z_lo + aw // tn

    def tile_of(j):
        return _skip_mlstm_k(j, mq_hi, mv_lo)

    def w_row(j):
        return pl.multiple_of(j * tn + jnp.where(j >= q_lo, N_GATES, 0), N_GATES)

    pb_written = [(0, mq_hi, -(mw // tn)), (mv_lo, oz_lo, mv_lo), (q_lo, z_lo, q_lo - (mw + qw) // tn)]
    pf_written = [(oz_lo, q_lo, oz_lo), (z_lo, n_j, oz_lo + z_lo - q_lo)]
    kern = functools.partial(_in_proj_kernel, mq_hi=mq_hi, mq_scale=(qw // M_HEADS) ** -0.5, mv_lo=mv_lo,
                             oz_lo=oz_lo, q_lo=q_lo, k_lo=k_lo, v_lo=v_lo, z_lo=z_lo)
    pb, pf = pl.pallas_call(
        kern,
        grid=(n_j - (mv_lo - mq_hi), n_i),
        in_specs=[
            pl.BlockSpec((tm, d), lambda j, i: (i, 0)),
            pl.BlockSpec((None, pl.Element(tn), pl.Element(d)), lambda j, i: (layer, w_row(tile_of(j)), 0)),
            pl.BlockSpec((tm, HEAD_DIM), lambda j, i: (i % tiles_per_batch, 0)),
            pl.BlockSpec((tm, HEAD_DIM), lambda j, i: (i % tiles_per_batch, 0)),
            pl.BlockSpec((1, HEAD_DIM), lambda j, i: (0, 0)),
            pl.BlockSpec((1, HEAD_DIM), lambda j, i: (0, 0)),
        ],
        out_specs=[
            pl.BlockSpec((tm, tn), lambda j, i: _held_block_index(tile_of(j), i, n_i, pb_written)),
            pl.BlockSpec((tm, tn), lambda j, i: _held_block_index(tile_of(j), i, n_i, pf_written)),
        ],
        out_shape=[
            jax.ShapeDtypeStruct((m, mw + qw + aw + 2 * kvw), BF16),
            jax.ShapeDtypeStruct((m, 2 * mw + aw), F32),
        ],
        scratch_shapes=[pltpu.VMEM((tn, d), BF16)],
        compiler_params=_params("arbitrary", "arbitrary"),
        name="in_projection",
    )(h3.reshape(m, d), w_t, cos_t, sin_t, g_q.reshape(1, HEAD_DIM), g_k.reshape(1, HEAD_DIM))
    return pb.reshape(bsz, t_all, -1), pf.reshape(bsz, t_all, -1)


def _kt_proj_kernel(h_ref, w_ref, o_ref, wb_ref):
    @pl.when(jnp.logical_and(pl.program_id(0) == 0, pl.program_id(1) == 0))
    def _():
        wb_ref[...] = w_ref[...].astype(BF16)

    o_ref[...] = _dot_nt(wb_ref[...], h_ref[...]).astype(BF16)


def _mlstm_k_transposed(h3, w_t, layer, qw):
    bsz, t_all, d = h3.shape
    tk = _tile(t_all, 256, LANES)
    return pl.pallas_call(
        _kt_proj_kernel,
        grid=(bsz, t_all // tk),
        in_specs=[
            pl.BlockSpec((None, tk, d), lambda b, t: (b, t, 0)),
            pl.BlockSpec((None, qw, d), lambda b, t: (layer, 1, 0), pipeline_mode=pl.Buffered(1)),
        ],
        out_specs=pl.BlockSpec((None, qw, tk), lambda b, t: (b, 0, t)),
        out_shape=jax.ShapeDtypeStruct((bsz, qw, t_all), BF16),
        scratch_shapes=[pltpu.VMEM((qw, d), BF16)],
        compiler_params=_params("arbitrary", "arbitrary"),
        name="mlstm_k_transposed",
    )(h3, w_t)


def _mlstm_chunk(qb, kt, vx, a_row, cm_col, b_col, cx, m, reverse):
    L = qb.shape[0]
    dv = vx.shape[1] - LANES
    t_idx = lax.broadcasted_iota(jnp.int32, (L, L), 0)
    s_idx = lax.broadcasted_iota(jnp.int32, (L, L), 1)
    seen = s_idx >= t_idx if reverse else s_idx <= t_idx
    last = 0 if reverse else L - 1
    m_run = jnp.maximum(m, cm_col)
    s = _dot(qb, kt) * jnp.exp(jnp.where(seen, a_row - m_run, -jnp.inf))
    nd = jnp.exp(m - m_run) * _dot(qb, cx.astype(BF16)) + _dot(s.astype(BF16), vx)
    h = nd[:, :dv] / jnp.maximum(jnp.abs(nd[:, dv:dv + 1]), jnp.exp(-(b_col + m_run)))
    m_last = m_run[last:last + 1, :]
    kw = (kt.astype(F32) * jnp.exp(a_row - m_last)).astype(BF16)
    cx_new = jnp.exp(m - m_last) * cx + _dot(kw, vx)
    return h, cx_new, b_col[last:last + 1, :] + m_last


def _mlstm_streams(q_ref, kt_ref, v_ref, gcol_ref, grow_ref, state_refs, reverse, emit):
    bsz = q_ref.shape[0]
    dk = q_ref.shape[-1] // M_HEADS
    dv = v_ref.shape[-1] // M_HEADS
    ones = jnp.ones((q_ref.shape[1], LANES), BF16)

    @pl.when(pl.program_id(0) == 0)
    def _():
        for ref in state_refs:
            ref[...] = jnp.zeros_like(ref)

    for b in range(bsz):
        for hd in range(M_HEADS):
            cx_ref, m_ref = state_refs[2 * (b * M_HEADS + hd):2 * (b * M_HEADS + hd) + 2]
            st = (M_HEADS if reverse else 0) + hd
            vx = jnp.concatenate([v_ref[b, :, hd * dv:(hd + 1) * dv], ones], axis=1)
            h, cx_new, m_new = _mlstm_chunk(
                q_ref[b, :, hd * dk:(hd + 1) * dk], kt_ref[b, hd * dk:(hd + 1) * dk, :], vx,
                grow_ref[b, st:st + 1, :], gcol_ref[b, :, st:st + 1],
                gcol_ref[b, :, 2 * M_HEADS + st:2 * M_HEADS + st + 1],
                cx_ref[...], m_ref[...], reverse)
            cx_ref[...] = cx_new
            m_ref[...] = m_new
            emit(b, hd, h)


def _mlstm_fwd_kernel(q_ref, kt_ref, v_ref, gcol_ref, grow_ref, h_ref, *state_refs):
    dv = v_ref.shape[-1] // M_HEADS

    def emit(b, hd, h):
        h_ref[b, :, hd * dv:(hd + 1) * dv] = h

    _mlstm_streams(q_ref, kt_ref, v_ref, gcol_ref, grow_ref, state_refs, False, emit)


def _mlstm_bwd_kernel(q_ref, kt_ref, v_ref, gcol_ref, grow_ref, hf_ref, o_ref, z_ref, gm_ref,
                      out_ref, *state_refs):
    dv = v_ref.shape[-1] // M_HEADS

    def emit(b, hd, h):
        cols = slice(hd * dv, (hd + 1) * dv)
        hs = hf_ref[b, :, cols] + h
        hn = hs * lax.rsqrt(jnp.mean(hs * hs, axis=-1, keepdims=True) + EPS) * gm_ref[:, cols]
        gated = hn * jax.nn.sigmoid(o_ref[b, :, cols]) * _silu(z_ref[b, :, cols])
        out_ref[b, :, cols] = gated.astype(out_ref.dtype)

    _mlstm_streams(q_ref, kt_ref, v_ref, gcol_ref, grow_ref, state_refs, True, emit)


def _mlstm(pb, kt, pf, gcol, grow, g_mlstm, ctx_len):
    bsz, t_all, _ = pb.shape
    mw = g_mlstm.shape[-1]
    qw = mw // 2
    nc = t_all // CHUNK
    n_ctx = ctx_len // CHUNK
    dk, dv = qw // M_HEADS, mw // M_HEADS

    def rev_chunk(c):
        return jnp.where(c < n_ctx, n_ctx - 1 - c, nc - 1 - (c - n_ctx))

    def specs(chunk_of):
        return [
            pl.BlockSpec((bsz, CHUNK, qw), lambda c: (0, chunk_of(c), mw // qw)),
            pl.BlockSpec((bsz, qw, CHUNK), lambda c: (0, 0, chunk_of(c))),
            pl.BlockSpec((bsz, CHUNK, mw), lambda c: (0, chunk_of(c), 0)),
            pl.BlockSpec((bsz, CHUNK, N_GATES), lambda c: (0, chunk_of(c), 0)),
            pl.BlockSpec((bsz, 2 * M_HEADS, CHUNK), lambda c: (0, 0, chunk_of(c))),
        ]

    scratch = [pltpu.VMEM((dk, dv + LANES), F32), pltpu.VMEM((1, 1), F32)] * (bsz * M_HEADS)
    h_fwd = pl.pallas_call(
        _mlstm_fwd_kernel,
        grid=(nc,),
        in_specs=specs(lambda c: c),
        out_specs=pl.BlockSpec((bsz, CHUNK, mw), lambda c: (0, c, 0)),
        out_shape=jax.ShapeDtypeStruct((bsz, t_all, mw), F32),
        scratch_shapes=scratch,
        compiler_params=_params("arbitrary"),
        name="mlstm_fwd",
    )(pb, kt, pb, gcol, grow)
    return pl.pallas_call(
        _mlstm_bwd_kernel,
        grid=(nc,),
        in_specs=specs(rev_chunk) + [
            pl.BlockSpec((bsz, CHUNK, mw), lambda c: (0, rev_chunk(c), 0)),
            pl.BlockSpec((bsz, CHUNK, mw), lambda c: (0, rev_chunk(c), 0)),
            pl.BlockSpec((bsz, CHUNK, mw), lambda c: (0, rev_chunk(c), 1)),
            pl.BlockSpec((1, mw), lambda c: (0, 0)),
        ],
        out_specs=pl.BlockSpec((bsz, CHUNK, mw), lambda c: (0, rev_chunk(c), 0)),
        out_shape=jax.ShapeDtypeStruct((bsz, t_all, mw), BF16),
        scratch_shapes=scratch,
        compiler_params=_params("arbitrary"),
        name="mlstm_bwd",
    )(pb, kt, pb, gcol, grow, h_fwd, pf, pf, g_mlstm.reshape(1, mw))


def _attn_kernel(q_ref, k_ref, v_ref, z_ref, o_ref, vx_ref, s_ref, p_ref, *, ctx_len, key_chunk, update_ctx):
    tq = q_ref.shape[0]
    t_all = k_ref.shape[0]
    n_split = 2
    heads_per_split = GQA_GROUP // n_split
    rows_per_split = heads_per_split * tq

    @pl.when(pl.program_id(2) == 0)
    def _():
        vx_ref[:, :HEAD_DIM] = v_ref[...]
        vx_ref[:, HEAD_DIM:] = jnp.ones((t_all, HEAD_DIM), BF16)

    def attend(n_keys):
        n_tiles = n_keys // LANES
        for sp in range(n_split):
            q = jnp.concatenate([q_ref[:, g * HEAD_DIM:(g + 1) * HEAD_DIM]
                                 for g in range(sp * heads_per_split, (sp + 1) * heads_per_split)], axis=0)
            rows = slice(sp * rows_per_split, (sp + 1) * rows_per_split)
            for lo in range(0, n_keys, key_chunk):
                hi = min(lo + key_chunk, n_keys)
                s_ref[rows, lo:hi] = _dot_nt(q, k_ref[lo:hi, :])
        for sp in range(n_split):
            for rb in range(rows_per_split // SOFTMAX_ROWS):
                r0 = sp * rows_per_split + rb * SOFTMAX_ROWS
                rows = slice(r0, r0 + SOFTMAX_ROWS)
                m_lanes = s_ref[rows, 0:LANES]
                for t in range(1, n_tiles):
                    m_lanes = jnp.maximum(m_lanes, s_ref[rows, t * LANES:(t + 1) * LANES])
                m_rows = jnp.broadcast_to(jnp.max(m_lanes, axis=-1, keepdims=True), (SOFTMAX_ROWS, LANES))
                for t in range(n_tiles):
                    cols = slice(t * LANES, (t + 1) * LANES)
                    p_ref[rows, cols] = jnp.exp2(s_ref[rows, cols] - m_rows).astype(BF16)
            rows = slice(sp * rows_per_split, (sp + 1) * rows_per_split)
            ov = _dot(p_ref[rows, 0:n_keys], vx_ref[0:n_keys, :])
            o = ov[:, :HEAD_DIM] / ov[:, HEAD_DIM:HEAD_DIM + 1]
            for gl in range(heads_per_split):
                g = sp * heads_per_split + gl
                cols = slice(g * HEAD_DIM, (g + 1) * HEAD_DIM)
                o_ref[:, cols] = (o[gl * tq:(gl + 1) * tq] * _silu(z_ref[:, cols])).astype(o_ref.dtype)

    @pl.when(pl.program_id(2) == 0)
    def _():
        if update_ctx:
            attend(ctx_len)
        else:
            o_ref[...] = jnp.zeros_like(o_ref)

    @pl.when(pl.program_id(2) > 0)
    def _():
        attend(t_all)


def _attention(pb, pf, mw, aw, ctx_len, update_ctx):
    bsz, t_all, _ = pb.shape
    kvw = aw // GQA_GROUP
    kv_heads = kvw // HEAD_DIM
    gw = GQA_GROUP * HEAD_DIM
    tq = ctx_len
    assert (t_all - ctx_len) % tq == 0 and tq % SOFTMAX_ROWS == 0
    base = mw + mw // 2
    q_blk = base // gw
    k_blk = (base + aw) // HEAD_DIM
    v_blk = (base + aw + kvw) // HEAD_DIM
    z_blk = 2 * mw // gw
    kern = functools.partial(_attn_kernel, ctx_len=ctx_len, key_chunk=512, update_ctx=update_ctx)
    return pl.pallas_call(
        kern,
        grid=(bsz, kv_heads, t_all // tq),
        in_specs=[
            pl.BlockSpec((None, tq, gw), lambda b, h, i: (b, i, q_blk + h)),
            pl.BlockSpec((None, t_all, HEAD_DIM), lambda b, h, i: (b, 0, k_blk + h)),
            pl.BlockSpec((None, t_all, HEAD_DIM), lambda b, h, i: (b, 0, v_blk + h)),
            pl.BlockSpec((None, tq, gw), lambda b, h, i: (b, i, z_blk + h)),
        ],
        out_specs=pl.BlockSpec((None, tq, gw), lambda b, h, i: (b, i, h)),
        out_shape=jax.ShapeDtypeStruct((bsz, t_all, aw), BF16),
        scratch_shapes=[pltpu.VMEM((t_all, 2 * HEAD_DIM), BF16),
                        pltpu.VMEM((GQA_GROUP * tq, t_all), F32),
                        pltpu.VMEM((GQA_GROUP * tq, t_all), BF16)],
        compiler_params=_params("arbitrary", "arbitrary", "arbitrary"),
        name="attention",
    )(pb, pb, pb, pf)


def _out_proj_kernel(am_ref, aa_ref, wm_ref, wa_ref, x_ref, gl_ref, gc_ref, o_ref, *, ctx_len, tiles_per_batch):
    tm = x_ref.shape[0]
    y = _dot(am_ref[...], wm_ref[...].astype(BF16)) + _dot(aa_ref[...], wa_ref[...].astype(BF16))
    row0 = (pl.program_id(0) % tiles_per_batch) * tm
    row = row0 + lax.broadcasted_iota(jnp.int32, y.shape, 0)
    gate = jnp.where(row < ctx_len, gc_ref[...], gl_ref[...])
    o_ref[...] = x_ref[...] + gate * y


def _out_projection(m_out, a_out, w_out, layer, x_all, mod_rows, ctx_len):
    bsz, t_all, d = x_all.shape
    mw = m_out.shape[-1]
    aw = a_out.shape[-1]
    assert mw == aw
    tm = _tile(t_all, 1152, 16)
    tpb = t_all // tm
    tn = _tile(d, 512, LANES)
    m = bsz * t_all
    kern = functools.partial(_out_proj_kernel, ctx_len=ctx_len, tiles_per_batch=tpb)
    out = pl.pallas_call(
        kern,
        grid=(m // tm, d // tn),
        in_specs=[
            pl.BlockSpec((tm, mw), lambda i, j: (i, 0)),
            pl.BlockSpec((tm, aw), lambda i, j: (i, 0)),
            pl.BlockSpec((None, mw, tn), lambda i, j: (layer, 0, j)),
            pl.BlockSpec((None, aw, tn), lambda i, j: (layer, 1, j)),
            pl.BlockSpec((tm, tn), lambda i, j: (i, j)),
            pl.BlockSpec((None, None, 1, tn), lambda i, j: (i // tpb, 2, 0, j)),
            pl.BlockSpec((None, None, 1, tn), lambda i, j: (bsz, 2, 0, j)),
        ],
        out_specs=pl.BlockSpec((tm, tn), lambda i, j: (i, j)),
        out_shape=jax.ShapeDtypeStruct((m, d), F32),
        compiler_params=_params("arbitrary", "arbitrary"),
        name="out_projection",
    )(m_out.reshape(m, mw), a_out.reshape(m, aw), w_out, w_out, x_all.reshape(m, d), mod_rows, mod_rows)
    return out.reshape(bsz, t_all, d)


def _final_norm_kernel(x_ref, g_ref, o_ref):
    x = x_ref[...]
    o_ref[...] = x * lax.rsqrt(jnp.mean(x * x, axis=-1, keepdims=True) + EPS) * g_ref[...]


def _final_norm(x_all, g_final, ctx_len):
    bsz, t_all, d = x_all.shape
    seq = t_all - ctx_len
    tm = _tile(math.gcd(ctx_len, seq), 256, 8)
    off = ctx_len // tm
    return pl.pallas_call(
        _final_norm_kernel,
        grid=(bsz, seq // tm),
        in_specs=[
            pl.BlockSpec((None, tm, d), lambda b, t: (b, t + off, 0)),
            pl.BlockSpec((1, d), lambda b, t: (0, 0)),
        ],
        out_specs=pl.BlockSpec((None, tm, d), lambda b, t: (b, t, 0)),
        out_shape=jax.ShapeDtypeStruct((bsz, seq, d), F32),
        compiler_params=_params("arbitrary", "arbitrary"),
        name="final_norm",
    )(x_all, g_final.reshape(1, d))


def kernel(x, c, ctx, c_ctx, w_mod, b_mod, g_norm, w_in, b_gate, g_mlstm, g_q, g_k, w_out, g_final):
    bsz, seq, d = x.shape
    ctx_len = ctx.shape[1]
    depth = w_mod.shape[0]
    mw = g_mlstm.shape[-1]
    aw = w_out.shape[1] - mw
    gate_col = 4 * mw
    assert mw == aw and w_in.shape[-1] == gate_col + N_GATES + aw * 5 // 2
    assert ctx_len % CHUNK == 0 and seq % CHUNK == 0 and seq % GRID_W == 0

    w_t = jnp.swapaxes(w_in, 1, 2)
    cos_t, sin_t = _rope_tables(seq, ctx_len)

    c_rows = jnp.concatenate([c, c_ctx[None, :]], axis=0)
    n_rows = bsz + 1
    c_rows = jnp.pad(c_rows, ((0, -n_rows % 8), (0, 0)))
    mod = _modulation(c_rows, w_mod, b_mod)
    mod = mod[:, :n_rows].reshape(depth, n_rows, 3, 1, d)

    x_all = None
    for layer in range(depth):
        update_ctx = layer < depth - 1
        prologue_args = (mod[layer], g_norm[layer], w_t, layer, gate_col, b_gate[layer], ctx_len)
        if layer == 0:
            h, gcol, grow, x_all = _prologue((ctx, x), *prologue_args)
        else:
            h, gcol, grow = _prologue(x_all, *prologue_args)
        pb, pf = _in_projection(h, w_t, layer, cos_t, sin_t, g_q[layer], g_k[layer], mw, aw)
        kt = _mlstm_k_transposed(h, w_t, layer, mw // 2)
        m_out = _mlstm(pb, kt, pf, gcol, grow, g_mlstm[layer], ctx_len)
        a_out = _attention(pb, pf, mw, aw, ctx_len, update_ctx)
        x_all = _out_projection(m_out, a_out, w_out, layer, x_all, mod[layer], ctx_len)
    return _final_norm(x_all, g_final, ctx_len)
```

```python
import functools
import math

import jax
import jax.numpy as jnp
from jax import lax
from jax.experimental import pallas as pl
from jax.experimental.pallas import tpu as pltpu

CHUNK = 256
M_HEADS = 4
HEAD_DIM = 128
GQA_GROUP = 4
GRID_W = 64
ROPE_THETA = 10000.0
EPS = 1e-6
N_GATES = 4 * M_HEADS
LOG2_E = 1.4426950408889634
SOFTMAX_ROWS = 128
ROTARY_ROW_PARTS = 4

LANES = 128
V7X_VMEM_LIMIT_BYTES = 56 * 1024 * 1024

F32 = jnp.float32
BF16 = jnp.bfloat16


def _params(*sem):
    return pltpu.CompilerParams(dimension_semantics=sem, vmem_limit_bytes=V7X_VMEM_LIMIT_BYTES)


def _tile(total, target, multiple):
    best = None
    for t in range(multiple, min(total, target) + 1, multiple):
        if total % t == 0:
            best = t
    assert best is not None, (total, target, multiple)
    return best


def _dot(a, b):
    return jnp.dot(a, b, preferred_element_type=F32)


def _dot_nt(a, b):
    return lax.dot_general(a, b, (((1,), (1,)), ((), ())), preferred_element_type=F32)


def _dot_tn(a, b):
    return lax.dot_general(a, b, (((0,), (0,)), ((), ())), preferred_element_type=F32)


def _silu(x):
    return x * jax.nn.sigmoid(x)


def _mod_kernel(c_ref, w_ref, b_ref, o_ref):
    @pl.when(pl.program_id(1) == 0)
    def _():
        o_ref[...] = jnp.broadcast_to(b_ref[...], o_ref.shape)

    o_ref[...] += _dot(_silu(c_ref[...]).astype(BF16), w_ref[...].astype(BF16))


def _modulation(c_rows, w_mod, b_mod):
    depth, d, n = w_mod.shape
    rows = c_rows.shape[0]
    tk = _tile(d, 256, LANES)
    c_chunks = c_rows.reshape(rows, d // tk, tk).swapaxes(0, 1)
    return pl.pallas_call(
        _mod_kernel,
        grid=(depth, d // tk),
        in_specs=[
            pl.BlockSpec((None, rows, tk), lambda l, k: (k, 0, 0)),
            pl.BlockSpec((None, tk, n), lambda l, k: (l, k, 0)),
            pl.BlockSpec((None, 1, n), lambda l, k: (l, 0, 0)),
        ],
        out_specs=pl.BlockSpec((None, rows, n), lambda l, k: (l, 0, 0)),
        out_shape=jax.ShapeDtypeStruct((depth, rows, n), F32),
        compiler_params=_params("arbitrary", "arbitrary"),
        name="modulation",
    )(c_chunks, w_mod, b_mod.reshape(depth, 1, n))


def _log_sigmoid(x):
    return jnp.minimum(x, 0.0) - jnp.log1p(jnp.exp(-jnp.abs(x)))


def _prologue_first_kernel(ctx_ref, lat_ref, g_ref, scale_ref, shift_ref, wg_ref, bgr_ref, bgc_ref,
                           h_ref, gcol_ref, grow_ref, xall_ref, *, n_ctx_tiles):
    @pl.when(pl.program_id(1) < n_ctx_tiles)
    def _():
        xall_ref[...] = ctx_ref[...]

    @pl.when(pl.program_id(1) >= n_ctx_tiles)
    def _():
        xall_ref[...] = lat_ref[...]

    _prologue_kernel(xall_ref, g_ref, scale_ref, shift_ref, wg_ref, bgr_ref, bgc_ref, h_ref, gcol_ref, grow_ref)


def _prologue_kernel(x_ref, g_ref, scale_ref, shift_ref, wg_ref, bgr_ref, bgc_ref, h_ref, gcol_ref, grow_ref):
    x = x_ref[...]
    y = x * lax.rsqrt(jnp.mean(x * x, axis=-1, keepdims=True) + EPS)
    h = (y * g_ref[...]) * (1.0 + scale_ref[...]) + shift_ref[...]
    hb = h.astype(BF16)
    h_ref[...] = hb
    _scan_gate_terms(hb, wg_ref[...], bgr_ref[...], bgc_ref[...], gcol_ref, grow_ref)


def _split3(x):
    hi = x.astype(BF16)
    rest = x - hi.astype(F32)
    mid = rest.astype(BF16)
    return hi, mid, (rest - mid.astype(F32)).astype(BF16)


def _scan_gate_terms(hb, wg, bg_row, bg_col, gcol_ref, grow_ref):
    H = M_HEADS
    tm = hb.shape[0]
    wi = jnp.concatenate([wg[0:H], wg[2 * H:3 * H]], axis=0).astype(BF16)
    wf = jnp.concatenate([wg[H:2 * H], wg[3 * H:4 * H]], axis=0).astype(BF16)
    bi_r = jnp.concatenate([bg_row[:, 0:H], bg_row[:, 2 * H:3 * H]], axis=1)
    bf_r = jnp.concatenate([bg_row[:, H:2 * H], bg_row[:, 3 * H:4 * H]], axis=1)
    bi_c = jnp.concatenate([bg_col[0:H], bg_col[2 * H:3 * H]], axis=0)
    bf_c = jnp.concatenate([bg_col[H:2 * H], bg_col[3 * H:4 * H]], axis=0)
    f_col = _log_sigmoid(_dot_nt(hb, wf) + bf_r)
    i_row = _dot_nt(wi, hb) + bi_c
    f_row = _log_sigmoid(_dot_nt(wf, hb) + bf_c)
    t_idx = lax.broadcasted_iota(jnp.int32, (tm, tm), 0)
    s_idx = lax.broadcasted_iota(jnp.int32, (tm, tm), 1)
    same_chunk = (t_idx // CHUNK) == (s_idx // CHUNK)
    at_or_before = jnp.logical_and(same_chunk, s_idx <= t_idx)
    at_or_after = jnp.logical_and(same_chunk, s_idx >= t_idx)
    lower = jnp.where(at_or_before, 1.0, 0.0).astype(BF16)
    upper = jnp.where(at_or_after, 1.0, 0.0).astype(BF16)

    def cumulate(parts, mat, left):
        return sum(_dot(mat, p) if left else _dot(p, mat) for p in parts)

    f_col_parts, f_row_parts = _split3(f_col), _split3(f_row)
    bwd_c = lax.broadcasted_iota(jnp.int32, (tm, 2 * H), 1) >= H
    bwd_r = lax.broadcasted_iota(jnp.int32, (2 * H, tm), 0) >= H
    b_col = jnp.where(bwd_c, cumulate(f_col_parts, upper, True), cumulate(f_col_parts, lower, True))
    b_row = jnp.where(bwd_r, cumulate(f_row_parts, lower, False), cumulate(f_row_parts, upper, False))
    a_row = i_row - b_row
    cm_cols = []
    for st in range(2 * H):
        seen = at_or_after if st >= H else at_or_before
        cm_cols.append(jnp.max(jnp.where(seen, a_row[st:st + 1, :], -jnp.inf), axis=1, keepdims=True))
    gcol_ref[...] = jnp.concatenate(cm_cols + [b_col], axis=1)
    grow_ref[...] = a_row


def _prologue(tokens, mod_rows, g_norm, w_t, layer, gate_row, b_gate, ctx_len):
    first = isinstance(tokens, tuple)
    if first:
        ctx, lat = tokens
        bsz, seq, d = lat.shape
        t_all = ctx_len + seq
    else:
        bsz, t_all, d = tokens.shape
    tm = _tile(math.gcd(ctx_len, t_all - ctx_len), 256, LANES)
    n_ctx_tiles = ctx_len // tm
    assert gate_row % N_GATES == 0

    def mod_row(b, t):
        return jnp.where(t < n_ctx_tiles, bsz, b)

    tile_spec = pl.BlockSpec((None, tm, d), lambda b, t: (b, t, 0))
    if first:
        kern = functools.partial(_prologue_first_kernel, n_ctx_tiles=n_ctx_tiles)
        token_specs = [
            pl.BlockSpec((None, tm, d), lambda b, t: (b, jnp.minimum(t, n_ctx_tiles - 1), 0)),
            pl.BlockSpec((None, tm, d), lambda b, t: (b, jnp.maximum(t - n_ctx_tiles, 0), 0)),
        ]
        token_args = [ctx, lat]
    else:
        kern, token_specs, token_args = _prologue_kernel, [tile_spec], [tokens]
    out_specs = [tile_spec,
                 pl.BlockSpec((None, tm, N_GATES), lambda b, t: (b, t, 0)),
                 pl.BlockSpec((None, 2 * M_HEADS, tm), lambda b, t: (b, 0, t))]
    out_shape = [jax.ShapeDtypeStruct((bsz, t_all, d), BF16),
                 jax.ShapeDtypeStruct((bsz, t_all, N_GATES), F32),
                 jax.ShapeDtypeStruct((bsz, 2 * M_HEADS, t_all), F32)]
    if first:
        out_specs.append(tile_spec)
        out_shape.append(jax.ShapeDtypeStruct((bsz, t_all, d), F32))
    return pl.pallas_call(
        kern,
        grid=(bsz, t_all // tm),
        in_specs=token_specs + [
            pl.BlockSpec((1, d), lambda b, t: (0, 0)),
            pl.BlockSpec((None, None, 1, d), lambda b, t: (mod_row(b, t), 1, 0, 0)),
            pl.BlockSpec((None, None, 1, d), lambda b, t: (mod_row(b, t), 0, 0, 0)),
            pl.BlockSpec((None, N_GATES, d), lambda b, t: (layer, gate_row // N_GATES, 0)),
            pl.BlockSpec((1, N_GATES), lambda b, t: (0, 0)),
            pl.BlockSpec((N_GATES, 1), lambda b, t: (0, 0)),
        ],
        out_specs=out_specs,
        out_shape=out_shape,
        compiler_params=_params("arbitrary", "arbitrary"),
        name="prologue",
    )(*token_args, g_norm.reshape(1, d), mod_rows, mod_rows, w_t,
      b_gate.reshape(1, N_GATES), b_gate.reshape(N_GATES, 1))


def _rope_tables(seq, ctx_len):
    axis_dim = HEAD_DIM // 2
    rows = seq // GRID_W
    row_ids = jnp.repeat(jnp.arange(rows), GRID_W).astype(F32)
    col_ids = jnp.tile(jnp.arange(GRID_W), rows).astype(F32)
    inv = ROPE_THETA ** (-jnp.arange(0, axis_dim, 2, dtype=F32) / axis_dim)
    ang_r, ang_c = row_ids[:, None] * inv, col_ids[:, None] * inv
    cos_t = jnp.concatenate([jnp.cos(ang_r)] * 2 + [jnp.cos(ang_c)] * 2, axis=-1)
    sin_t = jnp.concatenate([-jnp.sin(ang_r), jnp.sin(ang_r), -jnp.sin(ang_c), jnp.sin(ang_c)], axis=-1)
    cos_t = jnp.concatenate([jnp.ones((ctx_len, HEAD_DIM), F32), cos_t], axis=0)
    sin_t = jnp.concatenate([jnp.zeros((ctx_len, HEAD_DIM), F32), sin_t], axis=0)
    return cos_t, sin_t


def _skip_mlstm_k(j, mq_hi, mv_lo):
    return j + jnp.where(j >= mq_hi, mv_lo - mq_hi, 0)


def _head_norm_rope(x, g, cos_t, sin_t):
    quarter = HEAD_DIM // 4
    lane = lax.broadcasted_iota(jnp.int32, x.shape, 1)
    first_half = (lane % (2 * quarter)) < quarter
    xn = x * lax.rsqrt(jnp.mean(x * x, axis=-1, keepdims=True) + EPS) * g
    swapped = jnp.where(first_half, pltpu.roll(xn, HEAD_DIM - quarter, axis=1), pltpu.roll(xn, quarter, axis=1))
    return xn * cos_t + swapped * sin_t


def _in_proj_kernel(a_ref, w_ref, cos_ref, sin_ref, gq_ref, gk_ref, pb_ref, pf_ref, wb_ref, *,
                    mq_hi, mq_scale, mv_lo, oz_lo, q_lo, k_lo, v_lo, z_lo):
    j = _skip_mlstm_k(pl.program_id(0), mq_hi, mv_lo)

    @pl.when(pl.program_id(1) == 0)
    def _():
        wb_ref[...] = w_ref[...].astype(BF16)

    is_rotated = jnp.logical_and(j >= q_lo, j < v_lo)

    @pl.when(jnp.logical_not(is_rotated))
    def _():
        y = _dot_nt(a_ref[...], wb_ref[...])

        @pl.when(jnp.logical_or(jnp.logical_and(j >= oz_lo, j < q_lo), j >= z_lo))
        def _():
            pf_ref[...] = y

        @pl.when(jnp.logical_or(j < oz_lo, jnp.logical_and(j >= v_lo, j < z_lo)))
        def _():
            pb_ref[...] = (y * jnp.where(j < mq_hi, mq_scale, 1.0)).astype(BF16)

    @pl.when(is_rotated)
    def _():
        is_q = j < k_lo
        gain = jnp.where(is_q, gq_ref[...], gk_ref[...])
        post_scale = jnp.where(is_q, HEAD_DIM ** -0.5 * LOG2_E, 1.0)
        part = a_ref.shape[0] // ROTARY_ROW_PARTS
        for p in range(ROTARY_ROW_PARTS):
            rows = slice(p * part, (p + 1) * part)
            y = _dot_nt(a_ref[rows, :], wb_ref[...])
            for hd in range(y.shape[-1] // HEAD_DIM):
                cols = slice(hd * HEAD_DIM, (hd + 1) * HEAD_DIM)
                rotated = _head_norm_rope(y[:, cols], gain, cos_ref[rows, :], sin_ref[rows, :])
                pb_ref[rows, cols] = (rotated * post_scale).astype(BF16)


def _held_block_index(j, i, n_i, written):
    (lo0, _, shift0) = written[0]
    row, col = jnp.where(j < lo0, 0, i), jnp.where(j < lo0, lo0 - shift0, j - shift0)
    for k, (lo, hi, shift) in enumerate(written):
        nxt = written[k + 1][0] if k + 1 < len(written) else None
        held = j >= hi if nxt is None else jnp.logical_and(j >= hi, j < nxt)
        row = jnp.where(held, n_i - 1, row)
        col = jnp.where(held, hi - 1 - shift, col)
        if k > 0:
            col = jnp.where(jnp.logical_and(j >= lo, j < hi), j - shift, col)
    return row, col


def _in_projection(h3, w_t, layer, cos_t, sin_t, g_q, g_k, mw, aw):
    bsz, t_all, d = h3.shape
    m = bsz * t_all
    kvw = aw // GQA_GROUP
    qw = mw // 2
    tn = _tile(kvw, 512, LANES)
    tm = _tile(t_all, 1152, 16)
    tiles_per_batch = t_all // tm
    n_i = m // tm
    mq_hi = qw // tn
    mv_lo = 2 * mq_hi
    oz_lo = 2 * mw // tn
    q_lo = 4 * mw // tn
    k_lo = q_lo + aw // tn
    v_lo = k_lo + kvw // tn
    z_lo = v_lo + kvw // tn
    n_j = z_lo + aw // tn

    def tile_of(j):
        return _skip_mlstm_k(j, mq_hi, mv_lo)

    def w_row(j):
        return pl.multiple_of(j * tn + jnp.where(j >= q_lo, N_GATES, 0), N_GATES)

    pb_written = [(0, mq_hi, -(mw // tn)), (mv_lo, oz_lo, mv_lo), (q_lo, z_lo, q_lo - (mw + qw) // tn)]
    pf_written = [(oz_lo, q_lo, oz_lo), (z_lo, n_j, oz_lo + z_lo - q_lo)]
    kern = functools.partial(_in_proj_kernel, mq_hi=mq_hi, mq_scale=(qw // M_HEADS) ** -0.5, mv_lo=mv_lo,
                             oz_lo=oz_lo, q_lo=q_lo, k_lo=k_lo, v_lo=v_lo, z_lo=z_lo)
    pb, pf = pl.pallas_call(
        kern,
        grid=(n_j - (mv_lo - mq_hi), n_i),
        in_specs=[
            pl.BlockSpec((tm, d), lambda j, i: (i, 0)),
            pl.BlockSpec((None, pl.Element(tn), pl.Element(d)), lambda j, i: (layer, w_row(tile_of(j)), 0)),
            pl.BlockSpec((tm, HEAD_DIM), lambda j, i: (i % tiles_per_batch, 0)),
            pl.BlockSpec((tm, HEAD_DIM), lambda j, i: (i % tiles_per_batch, 0)),
            pl.BlockSpec((1, HEAD_DIM), lambda j, i: (0, 0)),
            pl.BlockSpec((1, HEAD_DIM), lambda j, i: (0, 0)),
        ],
        out_specs=[
            pl.BlockSpec((tm, tn), lambda j, i: _held_block_index(tile_of(j), i, n_i, pb_written)),
            pl.BlockSpec((tm, tn), lambda j, i: _held_block_index(tile_of(j), i, n_i, pf_written)),
        ],
        out_shape=[
            jax.ShapeDtypeStruct((m, mw + qw + aw + 2 * kvw), BF16),
            jax.ShapeDtypeStruct((m, 2 * mw + aw), F32),
        ],
        scratch_shapes=[pltpu.VMEM((tn, d), BF16)],
        compiler_params=_params("arbitrary", "arbitrary"),
        name="in_projection",
    )(h3.reshape(m, d), w_t, cos_t, sin_t, g_q.reshape(1, HEAD_DIM), g_k.reshape(1, HEAD_DIM))
    return pb.reshape(bsz, t_all, -1), pf.reshape(bsz, t_all, -1)


def _kt_proj_kernel(h_ref, w_ref, o_ref, wb_ref):
    @pl.when(jnp.logical_and(pl.program_id(0) == 0, pl.program_id(1) == 0))
    def _():
        wb_ref[...] = w_ref[...].astype(BF16)

    o_ref[...] = _dot_nt(wb_ref[...], h_ref[...]).astype(BF16)


def _mlstm_k_transposed(h3, w_t, layer, qw):
    bsz, t_all, d = h3.shape
    tk = _tile(t_all, 256, LANES)
    return pl.pallas_call(
        _kt_proj_kernel,
        grid=(bsz, t_all // tk),
        in_specs=[
            pl.BlockSpec((None, tk, d), lambda b, t: (b, t, 0)),
            pl.BlockSpec((None, qw, d), lambda b, t: (layer, 1, 0), pipeline_mode=pl.Buffered(1)),
        ],
        out_specs=pl.BlockSpec((None, qw, tk), lambda b, t: (b, 0, t)),
        out_shape=jax.ShapeDtypeStruct((bsz, qw, t_all), BF16),
        scratch_shapes=[pltpu.VMEM((qw, d), BF16)],
        compiler_params=_params("arbitrary", "arbitrary"),
        name="mlstm_k_transposed",
    )(h3, w_t)


def _mlstm_chunk(qb, kt, vx, a_row, cm_col, b_col, cx, m, reverse):
    L = qb.shape[0]
    dv = vx.shape[1] - LANES
    t_idx = lax.broadcasted_iota(jnp.int32, (L, L), 0)
    s_idx = lax.broadcasted_iota(jnp.int32, (L, L), 1)
    seen = s_idx >= t_idx if reverse else s_idx <= t_idx
    last = 0 if reverse else L - 1
    m_run = jnp.maximum(m, cm_col)
    s = _dot(qb, kt) * jnp.exp(jnp.where(seen, a_row - m_run, -jnp.inf))
    nd = jnp.exp(m - m_run) * _dot(qb, cx.astype(BF16)) + _dot(s.astype(BF16), vx)
    h = nd[:, :dv] / jnp.maximum(jnp.abs(nd[:, dv:dv + 1]), jnp.exp(-(b_col + m_run)))
    m_last = m_run[last:last + 1, :]
    kw = (kt.astype(F32) * jnp.exp(a_row - m_last)).astype(BF16)
    cx_new = jnp.exp(m - m_last) * cx + _dot(kw, vx)
    return h, cx_new, b_col[last:last + 1, :] + m_last


def _mlstm_streams(q_ref, kt_ref, v_ref, gcol_ref, grow_ref, state_refs, reverse, emit):
    bsz = q_ref.shape[0]
    dk = q_ref.shape[-1] // M_HEADS
    dv = v_ref.shape[-1] // M_HEADS
    ones = jnp.ones((q_ref.shape[1], LANES), BF16)

    @pl.when(pl.program_id(0) == 0)
    def _():
        for ref in state_refs:
            ref[...] = jnp.zeros_like(ref)

    for b in range(bsz):
        for hd in range(M_HEADS):
            cx_ref, m_ref = state_refs[2 * (b * M_HEADS + hd):2 * (b * M_HEADS + hd) + 2]
            st = (M_HEADS if reverse else 0) + hd
            vx = jnp.concatenate([v_ref[b, :, hd * dv:(hd + 1) * dv], ones], axis=1)
            h, cx_new, m_new = _mlstm_chunk(
                q_ref[b, :, hd * dk:(hd + 1) * dk], kt_ref[b, hd * dk:(hd + 1) * dk, :], vx,
                grow_ref[b, st:st + 1, :], gcol_ref[b, :, st:st + 1],
                gcol_ref[b, :, 2 * M_HEADS + st:2 * M_HEADS + st + 1],
                cx_ref[...], m_ref[...], reverse)
            cx_ref[...] = cx_new
            m_ref[...] = m_new
            emit(b, hd, h)


def _mlstm_fwd_kernel(q_ref, kt_ref, v_ref, gcol_ref, grow_ref, h_ref, *state_refs):
    dv = v_ref.shape[-1] // M_HEADS

    def emit(b, hd, h):
        h_ref[b, :, hd * dv:(hd + 1) * dv] = h

    _mlstm_streams(q_ref, kt_ref, v_ref, gcol_ref, grow_ref, state_refs, False, emit)


def _mlstm_bwd_kernel(q_ref, kt_ref, v_ref, gcol_ref, grow_ref, hf_ref, o_ref, z_ref, gm_ref,
                      out_ref, *state_refs):
    dv = v_ref.shape[-1] // M_HEADS

    def emit(b, hd, h):
        cols = slice(hd * dv, (hd + 1) * dv)
        hs = hf_ref[b, :, cols] + h
        hn = hs * lax.rsqrt(jnp.mean(hs * hs, axis=-1, keepdims=True) + EPS) * gm_ref[:, cols]
        gated = hn * jax.nn.sigmoid(o_ref[b, :, cols]) * _silu(z_ref[b, :, cols])
        out_ref[b, :, cols] = gated.astype(out_ref.dtype)

    _mlstm_streams(q_ref, kt_ref, v_ref, gcol_ref, grow_ref, state_refs, True, emit)


def _mlstm(pb, kt, pf, gcol, grow, g_mlstm, ctx_len):
    bsz, t_all, _ = pb.shape
    mw = g_mlstm.shape[-1]
    qw = mw // 2
    nc = t_all // CHUNK
    n_ctx = ctx_len // CHUNK
    dk, dv = qw // M_HEADS, mw // M_HEADS

    def rev_chunk(c):
        return jnp.where(c < n_ctx, n_ctx - 1 - c, nc - 1 - (c - n_ctx))

    def specs(chunk_of):
        return [
            pl.BlockSpec((bsz, CHUNK, qw), lambda c: (0, chunk_of(c), mw // qw)),
            pl.BlockSpec((bsz, qw, CHUNK), lambda c: (0, 0, chunk_of(c))),
            pl.BlockSpec((bsz, CHUNK, mw), lambda c: (0, chunk_of(c), 0)),
            pl.BlockSpec((bsz, CHUNK, N_GATES), lambda c: (0, chunk_of(c), 0)),
            pl.BlockSpec((bsz, 2 * M_HEADS, CHUNK), lambda c: (0, 0, chunk_of(c))),
        ]

    scratch = [pltpu.VMEM((dk, dv + LANES), F32), pltpu.VMEM((1, 1), F32)] * (bsz * M_HEADS)
    h_fwd = pl.pallas_call(
        _mlstm_fwd_kernel,
        grid=(nc,),
        in_specs=specs(lambda c: c),
        out_specs=pl.BlockSpec((bsz, CHUNK, mw), lambda c: (0, c, 0)),
        out_shape=jax.ShapeDtypeStruct((bsz, t_all, mw), F32),
        scratch_shapes=scratch,
        compiler_params=_params("arbitrary"),
        name="mlstm_fwd",
    )(pb, kt, pb, gcol, grow)
    return pl.pallas_call(
        _mlstm_bwd_kernel,
        grid=(nc,),
        in_specs=specs(rev_chunk) + [
            pl.BlockSpec((bsz, CHUNK, mw), lambda c: (0, rev_chunk(c), 0)),
            pl.BlockSpec((bsz, CHUNK, mw), lambda c: (0, rev_chunk(c), 0)),
            pl.BlockSpec((bsz, CHUNK, mw), lambda c: (0, rev_chunk(c), 1)),
            pl.BlockSpec((1, mw), lambda c: (0, 0)),
        ],
        out_specs=pl.BlockSpec((bsz, CHUNK, mw), lambda c: (0, rev_chunk(c), 0)),
        out_shape=jax.ShapeDtypeStruct((bsz, t_all, mw), BF16),
        scratch_shapes=scratch,
        compiler_params=_params("arbitrary"),
        name="mlstm_bwd",
    )(pb, kt, pb, gcol, grow, h_fwd, pf, pf, g_mlstm.reshape(1, mw))


def _attn_kernel(q_ref, k_ref, v_ref, z_ref, o_ref, vx_ref, *stage_refs, ctx_len, key_chunk, update_ctx):
    tq = q_ref.shape[0]
    t_all = k_ref.shape[0]
    n_split = len(stage_refs) // 2
    s_refs, p_refs = stage_refs[:n_split], stage_refs[n_split:]
    heads_per_split = GQA_GROUP // n_split
    rows_per_split = heads_per_split * tq

    @pl.when(pl.program_id(2) == 0)
    def _():
        vx_ref[:, :HEAD_DIM] = v_ref[...]
        vx_ref[:, HEAD_DIM:] = jnp.ones((t_all, HEAD_DIM), BF16)

    def attend(n_keys):
        n_tiles = n_keys // LANES

        def scores(sp):
            q = jnp.concatenate([q_ref[:, g * HEAD_DIM:(g + 1) * HEAD_DIM]
                                 for g in range(sp * heads_per_split, (sp + 1) * heads_per_split)], axis=0)
            for lo in range(0, n_keys, key_chunk):
                hi = min(lo + key_chunk, n_keys)
                s_refs[sp][:, lo:hi] = _dot_nt(q, k_ref[lo:hi, :])

        def softmax(sp):
            s_ref, p_ref = s_refs[sp], p_refs[sp]
            for rb in range(rows_per_split // SOFTMAX_ROWS):
                rows = slice(rb * SOFTMAX_ROWS, (rb + 1) * SOFTMAX_ROWS)
                m_lanes = s_ref[rows, 0:LANES]
                for t in range(1, n_tiles):
                    m_lanes = jnp.maximum(m_lanes, s_ref[rows, t * LANES:(t + 1) * LANES])
                m_rows = jnp.broadcast_to(jnp.max(m_lanes, axis=-1, keepdims=True), (SOFTMAX_ROWS, LANES))
                for t in range(n_tiles):
                    cols = slice(t * LANES, (t + 1) * LANES)
                    p_ref[rows, cols] = jnp.exp2(s_ref[rows, cols] - m_rows).astype(BF16)

        def values(sp):
            ov = _dot(p_refs[sp][:, 0:n_keys], vx_ref[0:n_keys, :])
            o = ov[:, :HEAD_DIM] / ov[:, HEAD_DIM:HEAD_DIM + 1]
            for gl in range(heads_per_split):
                g = sp * heads_per_split + gl
                cols = slice(g * HEAD_DIM, (g + 1) * HEAD_DIM)
                o_ref[:, cols] = (o[gl * tq:(gl + 1) * tq] * _silu(z_ref[:, cols])).astype(o_ref.dtype)

        scores(0)
        for sp in range(n_split):
            softmax(sp)
            if sp + 1 < n_split:
                scores(sp + 1)
            values(sp)

    @pl.when(pl.program_id(2) == 0)
    def _():
        if update_ctx:
            attend(ctx_len)
        else:
            o_ref[...] = jnp.zeros_like(o_ref)

    @pl.when(pl.program_id(2) > 0)
    def _():
        attend(t_all)


def _attention(pb, pf, mw, aw, ctx_len, update_ctx):
    bsz, t_all, _ = pb.shape
    kvw = aw // GQA_GROUP
    kv_heads = kvw // HEAD_DIM
    gw = GQA_GROUP * HEAD_DIM
    tq = ctx_len
    assert (t_all - ctx_len) % tq == 0 and tq % SOFTMAX_ROWS == 0
    base = mw + mw // 2
    q_blk = base // gw
    k_blk = (base + aw) // HEAD_DIM
    v_blk = (base + aw + kvw) // HEAD_DIM
    z_blk = 2 * mw // gw
    n_split = 2
    kern = functools.partial(_attn_kernel, ctx_len=ctx_len, key_chunk=512, update_ctx=update_ctx)
    return pl.pallas_call(
        kern,
        grid=(bsz, kv_heads, t_all // tq),
        in_specs=[
            pl.BlockSpec((None, tq, gw), lambda b, h, i: (b, i, q_blk + h)),
            pl.BlockSpec((None, t_all, HEAD_DIM), lambda b, h, i: (b, 0, k_blk + h)),
            pl.BlockSpec((None, t_all, HEAD_DIM), lambda b, h, i: (b, 0, v_blk + h)),
            pl.BlockSpec((None, tq, gw), lambda b, h, i: (b, i, z_blk + h)),
        ],
        out_specs=pl.BlockSpec((None, tq, gw), lambda b, h, i: (b, i, h)),
        out_shape=jax.ShapeDtypeStruct((bsz, t_all, aw), BF16),
        scratch_shapes=([pltpu.VMEM((t_all, 2 * HEAD_DIM), BF16)]
                        + [pltpu.VMEM((GQA_GROUP * tq // n_split, t_all), F32)] * n_split
                        + [pltpu.VMEM((GQA_GROUP * tq // n_split, t_all), BF16)] * n_split),
        compiler_params=_params("arbitrary", "arbitrary", "arbitrary"),
        name="attention",
    )(pb, pb, pb, pf)


def _out_proj_kernel(am_ref, aa_ref, wm_ref, wa_ref, x_ref, gl_ref, gc_ref, o_ref, *, ctx_len, tiles_per_batch):
    tm = x_ref.shape[0]
    y = _dot(am_ref[...], wm_ref[...].astype(BF16)) + _dot(aa_ref[...], wa_ref[...].astype(BF16))
    row0 = (pl.program_id(0) % tiles_per_batch) * tm
    row = row0 + lax.broadcasted_iota(jnp.int32, y.shape, 0)
    gate = jnp.where(row < ctx_len, gc_ref[...], gl_ref[...])
    o_ref[...] = x_ref[...] + gate * y


def _out_projection(m_out, a_out, w_out, layer, x_all, mod_rows, ctx_len):
    bsz, t_all, d = x_all.shape
    mw = m_out.shape[-1]
    aw = a_out.shape[-1]
    assert mw == aw
    tm = _tile(t_all, 1152, 16)
    tpb = t_all // tm
    tn = _tile(d, 512, LANES)
    m = bsz * t_all
    kern = functools.partial(_out_proj_kernel, ctx_len=ctx_len, tiles_per_batch=tpb)
    out = pl.pallas_call(
        kern,
        grid=(m // tm, d // tn),
        in_specs=[
            pl.BlockSpec((tm, mw), lambda i, j: (i, 0)),
            pl.BlockSpec((tm, aw), lambda i, j: (i, 0)),
            pl.BlockSpec((None, mw, tn), lambda i, j: (layer, 0, j)),
            pl.BlockSpec((None, aw, tn), lambda i, j: (layer, 1, j)),
            pl.BlockSpec((tm, tn), lambda i, j: (i, j)),
            pl.BlockSpec((None, None, 1, tn), lambda i, j: (i // tpb, 2, 0, j)),
            pl.BlockSpec((None, None, 1, tn), lambda i, j: (bsz, 2, 0, j)),
        ],
        out_specs=pl.BlockSpec((tm, tn), lambda i, j: (i, j)),
        out_shape=jax.ShapeDtypeStruct((m, d), F32),
        compiler_params=_params("arbitrary", "arbitrary"),
        name="out_projection",
    )(m_out.reshape(m, mw), a_out.reshape(m, aw), w_out, w_out, x_all.reshape(m, d), mod_rows, mod_rows)
    return out.reshape(bsz, t_all, d)


def _final_norm_kernel(x_ref, g_ref, o_ref):
    x = x_ref[...]
    o_ref[...] = x * lax.rsqrt(jnp.mean(x * x, axis=-1, keepdims=True) + EPS) * g_ref[...]


def _final_norm(x_all, g_final, ctx_len):
    bsz, t_all, d = x_all.shape
    seq = t_all - ctx_len
    tm = _tile(math.gcd(ctx_len, seq), 256, 8)
    off = ctx_len // tm
    return pl.pallas_call(
        _final_norm_kernel,
        grid=(bsz, seq // tm),
        in_specs=[
            pl.BlockSpec((None, tm, d), lambda b, t: (b, t + off, 0)),
            pl.BlockSpec((1, d), lambda b, t: (0, 0)),
        ],
        out_specs=pl.BlockSpec((None, tm, d), lambda b, t: (b, t, 0)),
        out_shape=jax.ShapeDtypeStruct((bsz, seq, d), F32),
        compiler_params=_params("arbitrary", "arbitrary"),
        name="final_norm",
    )(x_all, g_final.reshape(1, d))


def kernel(x, c, ctx, c_ctx, w_mod, b_mod, g_norm, w_in, b_gate, g_mlstm, g_q, g_k, w_out, g_final):
    bsz, seq, d = x.shape
    ctx_len = ctx.shape[1]
    depth = w_mod.shape[0]
    mw = g_mlstm.shape[-1]
    aw = w_out.shape[1] - mw
    gate_col = 4 * mw
    assert mw == aw and w_in.shape[-1] == gate_col + N_GATES + aw * 5 // 2
    assert ctx_len % CHUNK == 0 and seq % CHUNK == 0 and seq % GRID_W == 0

    w_t = jnp.swapaxes(w_in, 1, 2)
    cos_t, sin_t = _rope_tables(seq, ctx_len)

    c_rows = jnp.concatenate([c, c_ctx[None, :]], axis=0)
    n_rows = bsz + 1
    c_rows = jnp.pad(c_rows, ((0, -n_rows % 8), (0, 0)))
    mod = _modulation(c_rows, w_mod, b_mod)
    mod = mod[:, :n_rows].reshape(depth, n_rows, 3, 1, d)

    x_all = None
    for layer in range(depth):
        update_ctx = layer < depth - 1
        prologue_args = (mod[layer], g_norm[layer], w_t, layer, gate_col, b_gate[layer], ctx_len)
        if layer == 0:
            h, gcol, grow, x_all = _prologue((ctx, x), *prologue_args)
        else:
            h, gcol, grow = _prologue(x_all, *prologue_args)
        pb, pf = _in_projection(h, w_t, layer, cos_t, sin_t, g_q[layer], g_k[layer], mw, aw)
        kt = _mlstm_k_transposed(h, w_t, layer, mw // 2)
        m_out = _mlstm(pb, kt, pf, gcol, grow, g_mlstm[layer], ctx_len)
        a_out = _attention(pb, pf, mw, aw, ctx_len, update_ctx)
        x_all = _out_projection(m_out, a_out, w_out, layer, x_all, mod[layer], ctx_len)
    return _final_norm(x_all, g_final, ctx_len)
```

```python
import functools
import math

import jax
import jax.numpy as jnp
from jax import lax
from jax.experimental import pallas as pl
from jax.experimental.pallas import tpu as pltpu

CHUNK = 256
M_HEADS = 4
HEAD_DIM = 128
GQA_GROUP = 4
GRID_W = 64
ROPE_THETA = 10000.0
EPS = 1e-6
N_GATES = 4 * M_HEADS
LOG2_E = 1.4426950408889634
SOFTMAX_ROWS = 128
PROJECTION_ROW_PARTS = 4

LANES = 128
V7X_VMEM_LIMIT_BYTES = 56 * 1024 * 1024

F32 = jnp.float32
BF16 = jnp.bfloat16


def _params(*sem):
    return pltpu.CompilerParams(dimension_semantics=sem, vmem_limit_bytes=V7X_VMEM_LIMIT_BYTES)


def _tile(total, target, multiple):
    best = None
    for t in range(multiple, min(total, target) + 1, multiple):
        if total % t == 0:
            best = t
    assert best is not None, (total, target, multiple)
    return best


def _dot(a, b):
    return jnp.dot(a, b, preferred_element_type=F32)


def _dot_nt(a, b):
    return lax.dot_general(a, b, (((1,), (1,)), ((), ())), preferred_element_type=F32)


def _dot_tn(a, b):
    return lax.dot_general(a, b, (((0,), (0,)), ((), ())), preferred_element_type=F32)


def _silu(x):
    return x * jax.nn.sigmoid(x)


def _mod_kernel(c_ref, w_ref, b_ref, o_ref):
    @pl.when(pl.program_id(1) == 0)
    def _():
        o_ref[...] = jnp.broadcast_to(b_ref[...], o_ref.shape)

    o_ref[...] += _dot(_silu(c_ref[...]).astype(BF16), w_ref[...].astype(BF16))


def _modulation(c_rows, w_mod, b_mod):
    depth, d, n = w_mod.shape
    rows = c_rows.shape[0]
    tk = _tile(d, 256, LANES)
    c_chunks = c_rows.reshape(rows, d // tk, tk).swapaxes(0, 1)
    return pl.pallas_call(
        _mod_kernel,
        grid=(depth, d // tk),
        in_specs=[
            pl.BlockSpec((None, rows, tk), lambda l, k: (k, 0, 0)),
            pl.BlockSpec((None, tk, n), lambda l, k: (l, k, 0)),
            pl.BlockSpec((None, 1, n), lambda l, k: (l, 0, 0)),
        ],
        out_specs=pl.BlockSpec((None, rows, n), lambda l, k: (l, 0, 0)),
        out_shape=jax.ShapeDtypeStruct((depth, rows, n), F32),
        compiler_params=_params("arbitrary", "arbitrary"),
        name="modulation",
    )(c_chunks, w_mod, b_mod.reshape(depth, 1, n))


def _log_sigmoid(x):
    return jnp.minimum(x, 0.0) - jnp.log1p(jnp.exp(-jnp.abs(x)))


def _prologue_first_kernel(ctx_ref, lat_ref, g_ref, scale_ref, shift_ref, wg_ref, bg_ref,
                           h_ref, gcol_ref, grow_ref, xall_ref, *, n_ctx_tiles):
    @pl.when(pl.program_id(1) < n_ctx_tiles)
    def _():
        xall_ref[...] = ctx_ref[...]

    @pl.when(pl.program_id(1) >= n_ctx_tiles)
    def _():
        xall_ref[...] = lat_ref[...]

    _prologue_kernel(xall_ref, g_ref, scale_ref, shift_ref, wg_ref, bg_ref, h_ref, gcol_ref, grow_ref)


def _prologue_kernel(x_ref, g_ref, scale_ref, shift_ref, wg_ref, bg_ref, h_ref, gcol_ref, grow_ref):
    x = x_ref[...]
    y = x * lax.rsqrt(jnp.mean(x * x, axis=-1, keepdims=True) + EPS)
    h = (y * g_ref[...]) * (1.0 + scale_ref[...]) + shift_ref[...]
    hb = h.astype(BF16)
    h_ref[...] = hb
    _scan_gate_terms(hb, wg_ref[...], bg_ref[...], gcol_ref, grow_ref)


def _split3(x):
    hi = x.astype(BF16)
    rest = x - hi.astype(F32)
    mid = rest.astype(BF16)
    return hi, mid, (rest - mid.astype(F32)).astype(BF16)


def _scan_gate_terms(hb, wg, bg_row, gcol_ref, grow_ref):
    H = M_HEADS
    tm = hb.shape[0]
    lane = lax.broadcasted_iota(jnp.int32, (tm, LANES), 1)
    pre = _dot_nt(hb, wg.astype(BF16)) + bg_row
    gates = jnp.where((lane & H) != 0, _log_sigmoid(pre), pre)
    t_idx = lax.broadcasted_iota(jnp.int32, (tm, tm), 0)
    s_idx = lax.broadcasted_iota(jnp.int32, (tm, tm), 1)
    same_chunk = (t_idx // CHUNK) == (s_idx // CHUNK)
    at_or_before = jnp.logical_and(same_chunk, s_idx <= t_idx)
    at_or_after = jnp.logical_and(same_chunk, s_idx >= t_idx)
    lower = jnp.where(at_or_before, 1.0, 0.0).astype(BF16)
    upper = jnp.where(at_or_after, 1.0, 0.0).astype(BF16)
    parts = _split3(gates)
    sum_before = sum(_dot(lower, p) for p in parts)
    sum_after = sum(_dot(upper, p) for p in parts)
    b_col = jnp.where(lane >= 2 * H, sum_after, sum_before)
    gates_row, b_row = gates.T, b_col.T
    a_row = jnp.concatenate([gates_row[0:H] - b_row[H:2 * H],
                             gates_row[2 * H:3 * H] - b_row[3 * H:4 * H]], axis=0)
    cm_cols = []
    for st in range(2 * H):
        seen = at_or_after if st >= H else at_or_before
        cm_cols.append(jnp.max(jnp.where(seen, a_row[st:st + 1, :], -jnp.inf), axis=1, keepdims=True))
    gcol_ref[...] = jnp.concatenate(cm_cols + [b_col[:, H:2 * H], b_col[:, 3 * H:4 * H]], axis=1)
    grow_ref[...] = a_row


def _prologue(tokens, mod_rows, g_norm, w_t, layer, gate_row, b_gate, ctx_len):
    first = isinstance(tokens, tuple)
    if first:
        ctx, lat = tokens
        bsz, seq, d = lat.shape
        t_all = ctx_len + seq
    else:
        bsz, t_all, d = tokens.shape
    tm = _tile(math.gcd(ctx_len, t_all - ctx_len), 256, LANES)
    n_ctx_tiles = ctx_len // tm
    assert gate_row % LANES == 0 and gate_row + LANES <= w_t.shape[1]

    def mod_row(b, t):
        return jnp.where(t < n_ctx_tiles, bsz, b)

    tile_spec = pl.BlockSpec((None, tm, d), lambda b, t: (b, t, 0))
    if first:
        kern = functools.partial(_prologue_first_kernel, n_ctx_tiles=n_ctx_tiles)
        token_specs = [
            pl.BlockSpec((None, tm, d), lambda b, t: (b, jnp.minimum(t, n_ctx_tiles - 1), 0)),
            pl.BlockSpec((None, tm, d), lambda b, t: (b, jnp.maximum(t - n_ctx_tiles, 0), 0)),
        ]
        token_args = [ctx, lat]
    else:
        kern, token_specs, token_args = _prologue_kernel, [tile_spec], [tokens]
    out_specs = [tile_spec,
                 pl.BlockSpec((None, tm, N_GATES), lambda b, t: (b, t, 0)),
                 pl.BlockSpec((None, 2 * M_HEADS, tm), lambda b, t: (b, 0, t))]
    out_shape = [jax.ShapeDtypeStruct((bsz, t_all, d), BF16),
                 jax.ShapeDtypeStruct((bsz, t_all, N_GATES), F32),
                 jax.ShapeDtypeStruct((bsz, 2 * M_HEADS, t_all), F32)]
    if first:
        out_specs.append(tile_spec)
        out_shape.append(jax.ShapeDtypeStruct((bsz, t_all, d), F32))
    return pl.pallas_call(
        kern,
        grid=(bsz, t_all // tm),
        in_specs=token_specs + [
            pl.BlockSpec((1, d), lambda b, t: (0, 0)),
            pl.BlockSpec((None, None, 1, d), lambda b, t: (mod_row(b, t), 1, 0, 0)),
            pl.BlockSpec((None, None, 1, d), lambda b, t: (mod_row(b, t), 0, 0, 0)),
            pl.BlockSpec((None, LANES, d), lambda b, t: (layer, gate_row // LANES, 0)),
            pl.BlockSpec((1, LANES), lambda b, t: (0, 0)),
        ],
        out_specs=out_specs,
        out_shape=out_shape,
        compiler_params=_params("arbitrary", "arbitrary"),
        name="prologue",
    )(*token_args, g_norm.reshape(1, d), mod_rows, mod_rows, w_t,
      jnp.pad(b_gate, (0, LANES - N_GATES)).reshape(1, LANES))


def _rope_tables(seq, ctx_len):
    axis_dim = HEAD_DIM // 2
    rows = seq // GRID_W
    row_ids = jnp.repeat(jnp.arange(rows), GRID_W).astype(F32)
    col_ids = jnp.tile(jnp.arange(GRID_W), rows).astype(F32)
    inv = ROPE_THETA ** (-jnp.arange(0, axis_dim, 2, dtype=F32) / axis_dim)
    ang_r, ang_c = row_ids[:, None] * inv, col_ids[:, None] * inv
    cos_t = jnp.concatenate([jnp.cos(ang_r)] * 2 + [jnp.cos(ang_c)] * 2, axis=-1)
    sin_t = jnp.concatenate([-jnp.sin(ang_r), jnp.sin(ang_r), -jnp.sin(ang_c), jnp.sin(ang_c)], axis=-1)
    cos_t = jnp.concatenate([jnp.ones((ctx_len, HEAD_DIM), F32), cos_t], axis=0)
    sin_t = jnp.concatenate([jnp.zeros((ctx_len, HEAD_DIM), F32), sin_t], axis=0)
    return cos_t, sin_t


def _skip_mlstm_k(j, mq_hi, mv_lo):
    return j + jnp.where(j >= mq_hi, mv_lo - mq_hi, 0)


def _head_norm_rope(x, g, cos_t, sin_t):
    quarter = HEAD_DIM // 4
    lane = lax.broadcasted_iota(jnp.int32, x.shape, 1)
    first_half = (lane % (2 * quarter)) < quarter
    xn = x * lax.rsqrt(jnp.mean(x * x, axis=-1, keepdims=True) + EPS) * g
    swapped = jnp.where(first_half, pltpu.roll(xn, HEAD_DIM - quarter, axis=1), pltpu.roll(xn, quarter, axis=1))
    return xn * cos_t + swapped * sin_t


def _in_proj_kernel(a_ref, w_ref, cos_ref, sin_ref, gq_ref, gk_ref, pb_ref, pf_ref, wb_ref, *,
                    mq_hi, mq_scale, mv_lo, oz_lo, q_lo, k_lo, v_lo, z_lo):
    j = _skip_mlstm_k(pl.program_id(0), mq_hi, mv_lo)

    @pl.when(pl.program_id(1) == 0)
    def _():
        wb_ref[...] = w_ref[...].astype(BF16)

    def row_parts():
        part = a_ref.shape[0] // PROJECTION_ROW_PARTS
        for p in range(PROJECTION_ROW_PARTS):
            rows = slice(p * part, (p + 1) * part)
            yield rows, _dot_nt(a_ref[rows, :], wb_ref[...])

    @pl.when(jnp.logical_or(jnp.logical_and(j >= oz_lo, j < q_lo), j >= z_lo))
    def _():
        for rows, y in row_parts():
            pf_ref[rows, :] = y

    @pl.when(jnp.logical_or(j < oz_lo, jnp.logical_and(j >= v_lo, j < z_lo)))
    def _():
        scale = jnp.where(j < mq_hi, mq_scale, 1.0)
        for rows, y in row_parts():
            pb_ref[rows, :] = (y * scale).astype(BF16)

    @pl.when(jnp.logical_and(j >= q_lo, j < v_lo))
    def _():
        is_q = j < k_lo
        gain = jnp.where(is_q, gq_ref[...], gk_ref[...])
        post_scale = jnp.where(is_q, HEAD_DIM ** -0.5 * LOG2_E, 1.0)
        for rows, y in row_parts():
            for hd in range(y.shape[-1] // HEAD_DIM):
                cols = slice(hd * HEAD_DIM, (hd + 1) * HEAD_DIM)
                rotated = _head_norm_rope(y[:, cols], gain, cos_ref[rows, :], sin_ref[rows, :])
                pb_ref[rows, cols] = (rotated * post_scale).astype(BF16)


def _held_block_index(j, i, n_i, written):
    (lo0, _, shift0) = written[0]
    row, col = jnp.where(j < lo0, 0, i), jnp.where(j < lo0, lo0 - shift0, j - shift0)
    for k, (lo, hi, shift) in enumerate(written):
        nxt = written[k + 1][0] if k + 1 < len(written) else None
        held = j >= hi if nxt is None else jnp.logical_and(j >= hi, j < nxt)
        row = jnp.where(held, n_i - 1, row)
        col = jnp.where(held, hi - 1 - shift, col)
        if k > 0:
            col = jnp.where(jnp.logical_and(j >= lo, j < hi), j - shift, col)
    return row, col


def _in_projection(h3, w_t, layer, cos_t, sin_t, g_q, g_k, mw, aw):
    bsz, t_all, d = h3.shape
    m = bsz * t_all
    kvw = aw // GQA_GROUP
    qw = mw // 2
    tn = _tile(kvw, 512, LANES)
    tm = _tile(t_all, 1152, 16)
    tiles_per_batch = t_all // tm
    n_i = m // tm
    mq_hi = qw // tn
    mv_lo = 2 * mq_hi
    oz_lo = 2 * mw // tn
    q_lo = 4 * mw // tn
    k_lo = q_lo + aw // tn
    v_lo = k_lo + kvw // tn
    z_lo = v_lo + kvw // tn
    n_j = z_lo + aw // tn

    def tile_of(j):
        return _skip_mlstm_k(j, mq_hi, mv_lo)

    def w_row(j):
        return pl.multiple_of(j * tn + jnp.where(j >= q_lo, N_GATES, 0), N_GATES)

    pb_written = [(0, mq_hi, -(mw // tn)), (mv_lo, oz_lo, mv_lo), (q_lo, z_lo, q_lo - (mw + qw) // tn)]
    pf_written = [(oz_lo, q_lo, oz_lo), (z_lo, n_j, oz_lo + z_lo - q_lo)]
    kern = functools.partial(_in_proj_kernel, mq_hi=mq_hi, mq_scale=(qw // M_HEADS) ** -0.5, mv_lo=mv_lo,
                             oz_lo=oz_lo, q_lo=q_lo, k_lo=k_lo, v_lo=v_lo, z_lo=z_lo)
    pb, pf = pl.pallas_call(
        kern,
        grid=(n_j - (mv_lo - mq_hi), n_i),
        in_specs=[
            pl.BlockSpec((tm, d), lambda j, i: (i, 0)),
            pl.BlockSpec((None, pl.Element(tn), pl.Element(d)), lambda j, i: (layer, w_row(tile_of(j)), 0)),
            pl.BlockSpec((tm, HEAD_DIM), lambda j, i: (i % tiles_per_batch, 0)),
            pl.BlockSpec((tm, HEAD_DIM), lambda j, i: (i % tiles_per_batch, 0)),
            pl.BlockSpec((1, HEAD_DIM), lambda j, i: (0, 0)),
            pl.BlockSpec((1, HEAD_DIM), lambda j, i: (0, 0)),
        ],
        out_specs=[
            pl.BlockSpec((tm, tn), lambda j, i: _held_block_index(tile_of(j), i, n_i, pb_written)),
            pl.BlockSpec((tm, tn), lambda j, i: _held_block_index(tile_of(j), i, n_i, pf_written)),
        ],
        out_shape=[
            jax.ShapeDtypeStruct((m, mw + qw + aw + 2 * kvw), BF16),
            jax.ShapeDtypeStruct((m, 2 * mw + aw), F32),
        ],
        scratch_shapes=[pltpu.VMEM((tn, d), BF16)],
        compiler_params=_params("arbitrary", "arbitrary"),
        name="in_projection",
    )(h3.reshape(m, d), w_t, cos_t, sin_t, g_q.reshape(1, HEAD_DIM), g_k.reshape(1, HEAD_DIM))
    return pb.reshape(bsz, t_all, -1), pf.reshape(bsz, t_all, -1)


def _kt_proj_kernel(h_ref, w_ref, o_ref, wb_ref):
    @pl.when(jnp.logical_and(pl.program_id(0) == 0, pl.program_id(1) == 0))
    def _():
        wb_ref[...] = w_ref[...].astype(BF16)

    o_ref[...] = _dot_nt(wb_ref[...], h_ref[...]).astype(BF16)


def _mlstm_k_transposed(h3, w_t, layer, qw):
    bsz, t_all, d = h3.shape
    tk = _tile(t_all, 256, LANES)
    return pl.pallas_call(
        _kt_proj_kernel,
        grid=(bsz, t_all // tk),
        in_specs=[
            pl.BlockSpec((None, tk, d), lambda b, t: (b, t, 0)),
            pl.BlockSpec((None, qw, d), lambda b, t: (layer, 1, 0), pipeline_mode=pl.Buffered(1)),
        ],
        out_specs=pl.BlockSpec((None, qw, tk), lambda b, t: (b, 0, t)),
        out_shape=jax.ShapeDtypeStruct((bsz, qw, t_all), BF16),
        scratch_shapes=[pltpu.VMEM((qw, d), BF16)],
        compiler_params=_params("arbitrary", "arbitrary"),
        name="mlstm_k_transposed",
    )(h3, w_t)


def _mlstm_chunk(qb, kt, vx, a_row, cm_col, b_col, cx, m, reverse):
    L = qb.shape[0]
    dv = vx.shape[1] - LANES
    t_idx = lax.broadcasted_iota(jnp.int32, (L, L), 0)
    s_idx = lax.broadcasted_iota(jnp.int32, (L, L), 1)
    seen = s_idx >= t_idx if reverse else s_idx <= t_idx
    last = 0 if reverse else L - 1
    m_run = jnp.maximum(m, cm_col)
    s = _dot(qb, kt) * jnp.exp(jnp.where(seen, a_row - m_run, -jnp.inf))
    nd = jnp.exp(m - m_run) * _dot(qb, cx.astype(BF16)) + _dot(s.astype(BF16), vx)
    h = nd[:, :dv] / jnp.maximum(jnp.abs(nd[:, dv:dv + 1]), jnp.exp(-(b_col + m_run)))
    m_last = m_run[last:last + 1, :]
    kw = (kt.astype(F32) * jnp.exp(a_row - m_last)).astype(BF16)
    cx_new = jnp.exp(m - m_last) * cx + _dot(kw, vx)
    return h, cx_new, b_col[last:last + 1, :] + m_last


def _mlstm_streams(q_ref, kt_ref, v_ref, gcol_ref, grow_ref, state_refs, reverse, emit):
    bsz = q_ref.shape[0]
    dk = q_ref.shape[-1] // M_HEADS
    dv = v_ref.shape[-1] // M_HEADS
    ones = jnp.ones((q_ref.shape[1], LANES), BF16)

    @pl.when(pl.program_id(0) == 0)
    def _():
        for ref in state_refs:
            ref[...] = jnp.zeros_like(ref)

    for b in range(bsz):
        for hd in range(M_HEADS):
            cx_ref, m_ref = state_refs[2 * (b * M_HEADS + hd):2 * (b * M_HEADS + hd) + 2]
            st = (M_HEADS if reverse else 0) + hd
            vx = jnp.concatenate([v_ref[b, :, hd * dv:(hd + 1) * dv], ones], axis=1)
            h, cx_new, m_new = _mlstm_chunk(
                q_ref[b, :, hd * dk:(hd + 1) * dk], kt_ref[b, hd * dk:(hd + 1) * dk, :], vx,
                grow_ref[b, st:st + 1, :], gcol_ref[b, :, st:st + 1],
                gcol_ref[b, :, 2 * M_HEADS + st:2 * M_HEADS + st + 1],
                cx_ref[...], m_ref[...], reverse)
            cx_ref[...] = cx_new
            m_ref[...] = m_new
            emit(b, hd, h)


def _mlstm_fwd_kernel(q_ref, kt_ref, v_ref, gcol_ref, grow_ref, h_ref, *state_refs):
    dv = v_ref.shape[-1] // M_HEADS

    def emit(b, hd, h):
        h_ref[b, :, hd * dv:(hd + 1) * dv] = h

    _mlstm_streams(q_ref, kt_ref, v_ref, gcol_ref, grow_ref, state_refs, False, emit)


def _mlstm_bwd_kernel(q_ref, kt_ref, v_ref, gcol_ref, grow_ref, hf_ref, o_ref, z_ref, gm_ref,
                      out_ref, *state_refs):
    dv = v_ref.shape[-1] // M_HEADS

    def emit(b, hd, h):
        cols = slice(hd * dv, (hd + 1) * dv)
        hs = hf_ref[b, :, cols] + h
        hn = hs * lax.rsqrt(jnp.mean(hs * hs, axis=-1, keepdims=True) + EPS) * gm_ref[:, cols]
        gated = hn * jax.nn.sigmoid(o_ref[b, :, cols]) * _silu(z_ref[b, :, cols])
        out_ref[b, :, cols] = gated.astype(out_ref.dtype)

    _mlstm_streams(q_ref, kt_ref, v_ref, gcol_ref, grow_ref, state_refs, True, emit)


def _mlstm(pb, kt, pf, gcol, grow, g_mlstm, ctx_len):
    bsz, t_all, _ = pb.shape
    mw = g_mlstm.shape[-1]
    qw = mw // 2
    nc = t_all // CHUNK
    n_ctx = ctx_len // CHUNK
    dk, dv = qw // M_HEADS, mw // M_HEADS

    def rev_chunk(c):
        return jnp.where(c < n_ctx, n_ctx - 1 - c, nc - 1 - (c - n_ctx))

    def specs(chunk_of):
        return [
            pl.BlockSpec((bsz, CHUNK, qw), lambda c: (0, chunk_of(c), mw // qw)),
            pl.BlockSpec((bsz, qw, CHUNK), lambda c: (0, 0, chunk_of(c))),
            pl.BlockSpec((bsz, CHUNK, mw), lambda c: (0, chunk_of(c), 0)),
            pl.BlockSpec((bsz, CHUNK, N_GATES), lambda c: (0, chunk_of(c), 0)),
            pl.BlockSpec((bsz, 2 * M_HEADS, CHUNK), lambda c: (0, 0, chunk_of(c))),
        ]

    scratch = [pltpu.VMEM((dk, dv + LANES), F32), pltpu.VMEM((1, 1), F32)] * (bsz * M_HEADS)
    h_fwd = pl.pallas_call(
        _mlstm_fwd_kernel,
        grid=(nc,),
        in_specs=specs(lambda c: c),
        out_specs=pl.BlockSpec((bsz, CHUNK, mw), lambda c: (0, c, 0)),
        out_shape=jax.ShapeDtypeStruct((bsz, t_all, mw), F32),
        scratch_shapes=scratch,
        compiler_params=_params("arbitrary"),
        name="mlstm_fwd",
    )(pb, kt, pb, gcol, grow)
    return pl.pallas_call(
        _mlstm_bwd_kernel,
        grid=(nc,),
        in_specs=specs(rev_chunk) + [
            pl.BlockSpec((bsz, CHUNK, mw), lambda c: (0, rev_chunk(c), 0)),
            pl.BlockSpec((bsz, CHUNK, mw), lambda c: (0, rev_chunk(c), 0)),
            pl.BlockSpec((bsz, CHUNK, mw), lambda c: (0, rev_chunk(c), 1)),
            pl.BlockSpec((1, mw), lambda c: (0, 0)),
        ],
        out_specs=pl.BlockSpec((bsz, CHUNK, mw), lambda c: (0, rev_chunk(c), 0)),
        out_shape=jax.ShapeDtypeStruct((bsz, t_all, mw), BF16),
        scratch_shapes=scratch,
        compiler_params=_params("arbitrary"),
        name="mlstm_bwd",
    )(pb, kt, pb, gcol, grow, h_fwd, pf, pf, g_mlstm.reshape(1, mw))


def _attn_kernel(q_ref, k_ref, v_ref, z_ref, o_ref, vx_ref, *stage_refs, ctx_len, key_chunk, update_ctx):
    tq = q_ref.shape[0]
    t_all = k_ref.shape[0]
    n_split = len(stage_refs) // 2
    s_refs, p_refs = stage_refs[:n_split], stage_refs[n_split:]
    heads_per_split = GQA_GROUP // n_split
    rows_per_split = heads_per_split * tq

    @pl.when(pl.program_id(2) == 0)
    def _():
        vx_ref[:, :HEAD_DIM] = v_ref[...]
        vx_ref[:, HEAD_DIM:] = jnp.ones((t_all, HEAD_DIM), BF16)

    def attend(n_keys):
        n_tiles = n_keys // LANES

        def scores(sp):
            q = jnp.concatenate([q_ref[:, g * HEAD_DIM:(g + 1) * HEAD_DIM]
                                 for g in range(sp * heads_per_split, (sp + 1) * heads_per_split)], axis=0)
            for lo in range(0, n_keys, key_chunk):
                hi = min(lo + key_chunk, n_keys)
                s_refs[sp][:, lo:hi] = _dot_nt(q, k_ref[lo:hi, :])

        def softmax(sp):
            s_ref, p_ref = s_refs[sp], p_refs[sp]
            for rb in range(rows_per_split // SOFTMAX_ROWS):
                rows = slice(rb * SOFTMAX_ROWS, (rb + 1) * SOFTMAX_ROWS)
                m_lanes = s_ref[rows, 0:LANES]
                for t in range(1, n_tiles):
                    m_lanes = jnp.maximum(m_lanes, s_ref[rows, t * LANES:(t + 1) * LANES])
                m_rows = jnp.broadcast_to(jnp.max(m_lanes, axis=-1, keepdims=True), (SOFTMAX_ROWS, LANES))
                for t in range(n_tiles):
                    cols = slice(t * LANES, (t + 1) * LANES)
                    p_ref[rows, cols] = jnp.exp2(s_ref[rows, cols] - m_rows).astype(BF16)

        def values(sp):
            ov = _dot(p_refs[sp][:, 0:n_keys], vx_ref[0:n_keys, :])
            o = ov[:, :HEAD_DIM] / ov[:, HEAD_DIM:HEAD_DIM + 1]
            for gl in range(heads_per_split):
                g = sp * heads_per_split + gl
                cols = slice(g * HEAD_DIM, (g + 1) * HEAD_DIM)
                o_ref[:, cols] = (o[gl * tq:(gl + 1) * tq] * _silu(z_ref[:, cols])).astype(o_ref.dtype)

        scores(0)
        for sp in range(n_split):
            softmax(sp)
            if sp + 1 < n_split:
                scores(sp + 1)
            values(sp)

    @pl.when(pl.program_id(2) == 0)
    def _():
        if update_ctx:
            attend(ctx_len)
        else:
            o_ref[...] = jnp.zeros_like(o_ref)

    @pl.when(pl.program_id(2) > 0)
    def _():
        attend(t_all)


def _attention(pb, pf, mw, aw, ctx_len, update_ctx):
    bsz, t_all, _ = pb.shape
    kvw = aw // GQA_GROUP
    kv_heads = kvw // HEAD_DIM
    gw = GQA_GROUP * HEAD_DIM
    tq = ctx_len
    assert (t_all - ctx_len) % tq == 0 and tq % SOFTMAX_ROWS == 0
    base = mw + mw // 2
    q_blk = base // gw
    k_blk = (base + aw) // HEAD_DIM
    v_blk = (base + aw + kvw) // HEAD_DIM
    z_blk = 2 * mw // gw
    n_split = 2
    kern = functools.partial(_attn_kernel, ctx_len=ctx_len, key_chunk=512, update_ctx=update_ctx)
    return pl.pallas_call(
        kern,
        grid=(bsz, kv_heads, t_all // tq),
        in_specs=[
            pl.BlockSpec((None, tq, gw), lambda b, h, i: (b, i, q_blk + h)),
            pl.BlockSpec((None, t_all, HEAD_DIM), lambda b, h, i: (b, 0, k_blk + h)),
            pl.BlockSpec((None, t_all, HEAD_DIM), lambda b, h, i: (b, 0, v_blk + h)),
            pl.BlockSpec((None, tq, gw), lambda b, h, i: (b, i, z_blk + h)),
        ],
        out_specs=pl.BlockSpec((None, tq, gw), lambda b, h, i: (b, i, h)),
        out_shape=jax.ShapeDtypeStruct((bsz, t_all, aw), BF16),
        scratch_shapes=([pltpu.VMEM((t_all, 2 * HEAD_DIM), BF16)]
                        + [pltpu.VMEM((GQA_GROUP * tq // n_split, t_all), F32)] * n_split
                        + [pltpu.VMEM((GQA_GROUP * tq // n_split, t_all), BF16)] * n_split),
        compiler_params=_params("arbitrary", "arbitrary", "arbitrary"),
        name="attention",
    )(pb, pb, pb, pf)


def _out_proj_kernel(am_ref, aa_ref, wm_ref, wa_ref, x_ref, gl_ref, gc_ref, o_ref, *, ctx_len, tiles_per_batch):
    tm = x_ref.shape[0]
    wm, wa = wm_ref[...].astype(BF16), wa_ref[...].astype(BF16)
    row0 = (pl.program_id(0) % tiles_per_batch) * tm
    part = tm // PROJECTION_ROW_PARTS
    for p in range(PROJECTION_ROW_PARTS):
        rows = slice(p * part, (p + 1) * part)
        y = _dot(am_ref[rows, :], wm) + _dot(aa_ref[rows, :], wa)
        row = row0 + p * part + lax.broadcasted_iota(jnp.int32, y.shape, 0)
        gate = jnp.where(row < ctx_len, gc_ref[...], gl_ref[...])
        o_ref[rows, :] = x_ref[rows, :] + gate * y


def _out_projection(m_out, a_out, w_out, layer, x_all, mod_rows, ctx_len):
    bsz, t_all, d = x_all.shape
    mw = m_out.shape[-1]
    aw = a_out.shape[-1]
    assert mw == aw
    tm = _tile(t_all, 1152, 16)
    tpb = t_all // tm
    tn = _tile(d, 512, LANES)
    m = bsz * t_all
    kern = functools.partial(_out_proj_kernel, ctx_len=ctx_len, tiles_per_batch=tpb)
    out = pl.pallas_call(
        kern,
        grid=(m // tm, d // tn),
        in_specs=[
            pl.BlockSpec((tm, mw), lambda i, j: (i, 0)),
            pl.BlockSpec((tm, aw), lambda i, j: (i, 0)),
            pl.BlockSpec((None, mw, tn), lambda i, j: (layer, 0, j)),
            pl.BlockSpec((None, aw, tn), lambda i, j: (layer, 1, j)),
            pl.BlockSpec((tm, tn), lambda i, j: (i, j)),
            pl.BlockSpec((None, None, 1, tn), lambda i, j: (i // tpb, 2, 0, j)),
            pl.BlockSpec((None, None, 1, tn), lambda i, j: (bsz, 2, 0, j)),
        ],
        out_specs=pl.BlockSpec((tm, tn), lambda i, j: (i, j)),
        out_shape=jax.ShapeDtypeStruct((m, d), F32),
        compiler_params=_params("arbitrary", "arbitrary"),
        name="out_projection",
    )(m_out.reshape(m, mw), a_out.reshape(m, aw), w_out, w_out, x_all.reshape(m, d), mod_rows, mod_rows)
    return out.reshape(bsz, t_all, d)


def _final_norm_kernel(x_ref, g_ref, o_ref):
    x = x_ref[...]
    o_ref[...] = x * lax.rsqrt(jnp.mean(x * x, axis=-1, keepdims=True) + EPS) * g_ref[...]


def _final_norm(x_all, g_final, ctx_len):
    bsz, t_all, d = x_all.shape
    seq = t_all - ctx_len
    tm = _tile(math.gcd(ctx_len, seq), 256, 8)
    off = ctx_len // tm
    return pl.pallas_call(
        _final_norm_kernel,
        grid=(bsz, seq // tm),
        in_specs=[
            pl.BlockSpec((None, tm, d), lambda b, t: (b, t + off, 0)),
            pl.BlockSpec((1, d), lambda b, t: (0, 0)),
        ],
        out_specs=pl.BlockSpec((None, tm, d), lambda b, t: (b, t, 0)),
        out_shape=jax.ShapeDtypeStruct((bsz, seq, d), F32),
        compiler_params=_params("arbitrary", "arbitrary"),
        name="final_norm",
    )(x_all, g_final.reshape(1, d))


def kernel(x, c, ctx, c_ctx, w_mod, b_mod, g_norm, w_in, b_gate, g_mlstm, g_q, g_k, w_out, g_final):
    bsz, seq, d = x.shape
    ctx_len = ctx.shape[1]
    depth = w_mod.shape[0]
    mw = g_mlstm.shape[-1]
    aw = w_out.shape[1] - mw
    gate_col = 4 * mw
    assert mw == aw and w_in.shape[-1] == gate_col + N_GATES + aw * 5 // 2
    assert ctx_len % CHUNK == 0 and seq % CHUNK == 0 and seq % GRID_W == 0

    w_t = jnp.swapaxes(w_in, 1, 2)
    cos_t, sin_t = _rope_tables(seq, ctx_len)

    c_rows = jnp.concatenate([c, c_ctx[None, :]], axis=0)
    n_rows = bsz + 1
    c_rows = jnp.pad(c_rows, ((0, -n_rows % 8), (0, 0)))
    mod = _modulation(c_rows, w_mod, b_mod)
    mod = mod[:, :n_rows].reshape(depth, n_rows, 3, 1, d)

    x_all = None
    for layer in range(depth):
        update_ctx = layer < depth - 1
        prologue_args = (mod[layer], g_norm[layer], w_t, layer, gate_col, b_gate[layer], ctx_len)
        if layer == 0:
            h, gcol, grow, x_all = _prologue((ctx, x), *prologue_args)
        else:
            h, gcol, grow = _prologue(x_all, *prologue_args)
        pb, pf = _in_projection(h, w_t, layer, cos_t, sin_t, g_q[layer], g_k[layer], mw, aw)
        kt = _mlstm_k_transposed(h, w_t, layer, mw // 2)
        m_out = _mlstm(pb, kt, pf, gcol, grow, g_mlstm[layer], ctx_len)
        a_out = _attention(pb, pf, mw, aw, ctx_len, update_ctx)
        x_all = _out_projection(m_out, a_out, w_out, layer, x_all, mod[layer], ctx_len)
    return _final_norm(x_all, g_final, ctx_len)
```

```python
import functools
import math

import jax
import jax.numpy as jnp
from jax import lax
from jax.experimental import pallas as pl
from jax.experimental.pallas import tpu as pltpu

CHUNK = 256
M_HEADS = 4
HEAD_DIM = 128
GQA_GROUP = 4
GRID_W = 64
ROPE_THETA = 10000.0
EPS = 1e-6
N_GATES = 4 * M_HEADS
LOG2_E = 1.4426950408889634
SOFTMAX_ROWS = 128
PROJECTION_ROW_PARTS = 4

LANES = 128
V7X_VMEM_LIMIT_BYTES = 56 * 1024 * 1024

F32 = jnp.float32
BF16 = jnp.bfloat16


def _params(*sem):
    return pltpu.CompilerParams(dimension_semantics=sem, vmem_limit_bytes=V7X_VMEM_LIMIT_BYTES)


def _tile(total, target, multiple):
    best = None
    for t in range(multiple, min(total, target) + 1, multiple):
        if total % t == 0:
            best = t
    assert best is not None, (total, target, multiple)
    return best


def _dot(a, b):
    return jnp.dot(a, b, preferred_element_type=F32)


def _dot_nt(a, b):
    return lax.dot_general(a, b, (((1,), (1,)), ((), ())), preferred_element_type=F32)


def _dot_tn(a, b):
    return lax.dot_general(a, b, (((0,), (0,)), ((), ())), preferred_element_type=F32)


def _silu(x):
    return x * jax.nn.sigmoid(x)


def _mod_kernel(c_ref, w_ref, b_ref, o_ref):
    @pl.when(pl.program_id(1) == 0)
    def _():
        o_ref[...] = jnp.broadcast_to(b_ref[...], o_ref.shape)

    o_ref[...] += _dot(_silu(c_ref[...]).astype(BF16), w_ref[...].astype(BF16))


def _modulation(c_rows, w_mod, b_mod):
    depth, d, n = w_mod.shape
    rows = c_rows.shape[0]
    tk = _tile(d, 256, LANES)
    c_chunks = c_rows.reshape(rows, d // tk, tk).swapaxes(0, 1)
    return pl.pallas_call(
        _mod_kernel,
        grid=(depth, d // tk),
        in_specs=[
            pl.BlockSpec((None, rows, tk), lambda l, k: (k, 0, 0)),
            pl.BlockSpec((None, tk, n), lambda l, k: (l, k, 0)),
            pl.BlockSpec((None, 1, n), lambda l, k: (l, 0, 0)),
        ],
        out_specs=pl.BlockSpec((None, rows, n), lambda l, k: (l, 0, 0)),
        out_shape=jax.ShapeDtypeStruct((depth, rows, n), F32),
        compiler_params=_params("arbitrary", "arbitrary"),
        name="modulation",
    )(c_chunks, w_mod, b_mod.reshape(depth, 1, n))


def _log_sigmoid(x):
    return jnp.minimum(x, 0.0) - jnp.log1p(jnp.exp(-jnp.abs(x)))


def _prologue_first_kernel(ctx_ref, lat_ref, g_ref, scale_ref, shift_ref, wg_ref, bg_ref,
                           h_ref, gcol_ref, grow_ref, xall_ref, *, n_ctx_tiles):
    @pl.when(pl.program_id(1) < n_ctx_tiles)
    def _():
        xall_ref[...] = ctx_ref[...]

    @pl.when(pl.program_id(1) >= n_ctx_tiles)
    def _():
        xall_ref[...] = lat_ref[...]

    _prologue_kernel(xall_ref, g_ref, scale_ref, shift_ref, wg_ref, bg_ref, h_ref, gcol_ref, grow_ref)


def _prologue_kernel(x_ref, g_ref, scale_ref, shift_ref, wg_ref, bg_ref, h_ref, gcol_ref, grow_ref):
    x = x_ref[...]
    y = x * lax.rsqrt(jnp.mean(x * x, axis=-1, keepdims=True) + EPS)
    h = (y * g_ref[...]) * (1.0 + scale_ref[...]) + shift_ref[...]
    hb = h.astype(BF16)
    h_ref[...] = hb
    _scan_gate_terms(hb, wg_ref[...], bg_ref[...], gcol_ref, grow_ref)


def _split3(x):
    hi = x.astype(BF16)
    rest = x - hi.astype(F32)
    mid = rest.astype(BF16)
    return hi, mid, (rest - mid.astype(F32)).astype(BF16)


def _scan_gate_terms(hb, wg, bg_row, gcol_ref, grow_ref):
    H = M_HEADS
    tm = hb.shape[0]
    lane = lax.broadcasted_iota(jnp.int32, (tm, LANES), 1)
    pre = _dot_nt(hb, wg.astype(BF16)) + bg_row
    gates = jnp.where((lane & H) != 0, _log_sigmoid(pre), pre)
    t_idx = lax.broadcasted_iota(jnp.int32, (tm, tm), 0)
    s_idx = lax.broadcasted_iota(jnp.int32, (tm, tm), 1)
    same_chunk = (t_idx // CHUNK) == (s_idx // CHUNK)
    at_or_before = jnp.logical_and(same_chunk, s_idx <= t_idx)
    at_or_after = jnp.logical_and(same_chunk, s_idx >= t_idx)
    lower = jnp.where(at_or_before, 1.0, 0.0).astype(BF16)
    upper = jnp.where(at_or_after, 1.0, 0.0).astype(BF16)
    parts = _split3(gates)
    sum_before = sum(_dot(lower, p) for p in parts)
    sum_after = sum(_dot(upper, p) for p in parts)
    b_col = jnp.where(lane >= 2 * H, sum_after, sum_before)
    gates_row, b_row = gates.T, b_col.T
    a_row = jnp.concatenate([gates_row[0:H] - b_row[H:2 * H],
                             gates_row[2 * H:3 * H] - b_row[3 * H:4 * H]], axis=0)
    cm_cols = []
    for st in range(2 * H):
        seen = at_or_after if st >= H else at_or_before
        cm_cols.append(jnp.max(jnp.where(seen, a_row[st:st + 1, :], -jnp.inf), axis=1, keepdims=True))
    gcol_ref[...] = jnp.concatenate(cm_cols + [b_col[:, H:2 * H], b_col[:, 3 * H:4 * H]], axis=1)
    grow_ref[...] = a_row


def _prologue(tokens, mod_rows, g_norm, w_t, layer, gate_row, b_gate, ctx_len):
    first = isinstance(tokens, tuple)
    if first:
        ctx, lat = tokens
        bsz, seq, d = lat.shape
        t_all = ctx_len + seq
    else:
        bsz, t_all, d = tokens.shape
    tm = _tile(math.gcd(ctx_len, t_all - ctx_len), 256, LANES)
    n_ctx_tiles = ctx_len // tm
    assert gate_row % LANES == 0 and gate_row + LANES <= w_t.shape[1]

    def mod_row(b, t):
        return jnp.where(t < n_ctx_tiles, bsz, b)

    tile_spec = pl.BlockSpec((None, tm, d), lambda b, t: (b, t, 0))
    if first:
        kern = functools.partial(_prologue_first_kernel, n_ctx_tiles=n_ctx_tiles)
        token_specs = [
            pl.BlockSpec((None, tm, d), lambda b, t: (b, jnp.minimum(t, n_ctx_tiles - 1), 0)),
            pl.BlockSpec((None, tm, d), lambda b, t: (b, jnp.maximum(t - n_ctx_tiles, 0), 0)),
        ]
        token_args = [ctx, lat]
    else:
        kern, token_specs, token_args = _prologue_kernel, [tile_spec], [tokens]
    out_specs = [tile_spec,
                 pl.BlockSpec((None, tm, N_GATES), lambda b, t: (b, t, 0)),
                 pl.BlockSpec((None, 2 * M_HEADS, tm), lambda b, t: (b, 0, t))]
    out_shape = [jax.ShapeDtypeStruct((bsz, t_all, d), BF16),
                 jax.ShapeDtypeStruct((bsz, t_all, N_GATES), F32),
                 jax.ShapeDtypeStruct((bsz, 2 * M_HEADS, t_all), F32)]
    if first:
        out_specs.append(tile_spec)
        out_shape.append(jax.ShapeDtypeStruct((bsz, t_all, d), F32))
    return pl.pallas_call(
        kern,
        grid=(bsz, t_all // tm),
        in_specs=token_specs + [
            pl.BlockSpec((1, d), lambda b, t: (0, 0)),
            pl.BlockSpec((None, None, 1, d), lambda b, t: (mod_row(b, t), 1, 0, 0)),
            pl.BlockSpec((None, None, 1, d), lambda b, t: (mod_row(b, t), 0, 0, 0)),
            pl.BlockSpec((None, LANES, d), lambda b, t: (layer, gate_row // LANES, 0)),
            pl.BlockSpec((1, LANES), lambda b, t: (0, 0)),
        ],
        out_specs=out_specs,
        out_shape=out_shape,
        compiler_params=_params("arbitrary", "arbitrary"),
        name="prologue",
    )(*token_args, g_norm.reshape(1, d), mod_rows, mod_rows, w_t,
      jnp.pad(b_gate, (0, LANES - N_GATES)).reshape(1, LANES))


def _rope_tables(seq, ctx_len):
    axis_dim = HEAD_DIM // 2
    rows = seq // GRID_W
    row_ids = jnp.repeat(jnp.arange(rows), GRID_W).astype(F32)
    col_ids = jnp.tile(jnp.arange(GRID_W), rows).astype(F32)
    inv = ROPE_THETA ** (-jnp.arange(0, axis_dim, 2, dtype=F32) / axis_dim)
    ang_r, ang_c = row_ids[:, None] * inv, col_ids[:, None] * inv
    cos_t = jnp.concatenate([jnp.cos(ang_r)] * 2 + [jnp.cos(ang_c)] * 2, axis=-1)
    sin_t = jnp.concatenate([-jnp.sin(ang_r), jnp.sin(ang_r), -jnp.sin(ang_c), jnp.sin(ang_c)], axis=-1)
    cos_t = jnp.concatenate([jnp.ones((ctx_len, HEAD_DIM), F32), cos_t], axis=0)
    sin_t = jnp.concatenate([jnp.zeros((ctx_len, HEAD_DIM), F32), sin_t], axis=0)
    return cos_t, sin_t


def _skip_mlstm_k(j, mq_hi, mv_lo):
    return j + jnp.where(j >= mq_hi, mv_lo - mq_hi, 0)


def _head_norm_rope(x, g, cos_t, sin_t):
    quarter = HEAD_DIM // 4
    lane = lax.broadcasted_iota(jnp.int32, x.shape, 1)
    first_half = (lane % (2 * quarter)) < quarter
    xn = x * lax.rsqrt(jnp.mean(x * x, axis=-1, keepdims=True) + EPS) * g
    swapped = jnp.where(first_half, pltpu.roll(xn, HEAD_DIM - quarter, axis=1), pltpu.roll(xn, quarter, axis=1))
    return xn * cos_t + swapped * sin_t


def _in_proj_kernel(a_ref, w_ref, cos_ref, sin_ref, gq_ref, gk_ref, p_ref, wb_ref, *,
                    mq_hi, mq_scale, mv_lo, q_lo, k_lo, v_lo):
    j = _skip_mlstm_k(pl.program_id(0), mq_hi, mv_lo)

    @pl.when(pl.program_id(1) == 0)
    def _():
        wb_ref[...] = w_ref[...].astype(BF16)

    def row_parts():
        part = a_ref.shape[0] // PROJECTION_ROW_PARTS
        for p in range(PROJECTION_ROW_PARTS):
            rows = slice(p * part, (p + 1) * part)
            yield rows, _dot_nt(a_ref[rows, :], wb_ref[...])

    is_rotated = jnp.logical_and(j >= q_lo, j < v_lo)

    @pl.when(jnp.logical_not(is_rotated))
    def _():
        scale = jnp.where(j < mq_hi, mq_scale, 1.0)
        for rows, y in row_parts():
            p_ref[rows, :] = (y * scale).astype(BF16)

    @pl.when(is_rotated)
    def _():
        is_q = j < k_lo
        gain = jnp.where(is_q, gq_ref[...], gk_ref[...])
        post_scale = jnp.where(is_q, HEAD_DIM ** -0.5 * LOG2_E, 1.0)
        for rows, y in row_parts():
            for hd in range(y.shape[-1] // HEAD_DIM):
                cols = slice(hd * HEAD_DIM, (hd + 1) * HEAD_DIM)
                rotated = _head_norm_rope(y[:, cols], gain, cos_ref[rows, :], sin_ref[rows, :])
                p_ref[rows, cols] = (rotated * post_scale).astype(BF16)


def _in_projection(h3, w_t, layer, cos_t, sin_t, g_q, g_k, mw, aw):
    bsz, t_all, d = h3.shape
    m = bsz * t_all
    kvw = aw // GQA_GROUP
    qw = mw // 2
    tn = _tile(kvw, 512, LANES)
    tm = _tile(t_all, 1152, 16 * PROJECTION_ROW_PARTS)
    tiles_per_batch = t_all // tm
    mq_hi = qw // tn
    mv_lo = 2 * mq_hi
    q_lo = 4 * mw // tn
    k_lo = q_lo + aw // tn
    v_lo = k_lo + kvw // tn
    n_j = v_lo + (kvw + aw) // tn

    def tile_of(j):
        return _skip_mlstm_k(j, mq_hi, mv_lo)

    def w_row(j):
        return pl.multiple_of(j * tn + jnp.where(j >= q_lo, N_GATES, 0), N_GATES)

    def out_col(j):
        return jnp.where(j < mq_hi, j + (q_lo - mv_lo), jnp.where(j < q_lo, j - mv_lo, j - mq_hi))

    kern = functools.partial(_in_proj_kernel, mq_hi=mq_hi, mq_scale=(qw // M_HEADS) ** -0.5, mv_lo=mv_lo,
                             q_lo=q_lo, k_lo=k_lo, v_lo=v_lo)
    p = pl.pallas_call(
        kern,
        grid=(n_j - (mv_lo - mq_hi), m // tm),
        in_specs=[
            pl.BlockSpec((tm, d), lambda j, i: (i, 0)),
            pl.BlockSpec((None, pl.Element(tn), pl.Element(d)), lambda j, i: (layer, w_row(tile_of(j)), 0)),
            pl.BlockSpec((tm, HEAD_DIM), lambda j, i: (i % tiles_per_batch, 0)),
            pl.BlockSpec((tm, HEAD_DIM), lambda j, i: (i % tiles_per_batch, 0)),
            pl.BlockSpec((1, HEAD_DIM), lambda j, i: (0, 0)),
            pl.BlockSpec((1, HEAD_DIM), lambda j, i: (0, 0)),
        ],
        out_specs=pl.BlockSpec((tm, tn), lambda j, i: (i, out_col(tile_of(j)))),
        out_shape=jax.ShapeDtypeStruct((m, 3 * mw + qw + 2 * aw + 2 * kvw), BF16),
        scratch_shapes=[pltpu.VMEM((tn, d), BF16)],
        compiler_params=_params("arbitrary", "arbitrary"),
        name="in_projection",
    )(h3.reshape(m, d), w_t, cos_t, sin_t, g_q.reshape(1, HEAD_DIM), g_k.reshape(1, HEAD_DIM))
    return p.reshape(bsz, t_all, -1)


def _kt_proj_kernel(h_ref, w_ref, o_ref, wb_ref):
    @pl.when(jnp.logical_and(pl.program_id(0) == 0, pl.program_id(1) == 0))
    def _():
        wb_ref[...] = w_ref[...].astype(BF16)

    o_ref[...] = _dot_nt(wb_ref[...], h_ref[...]).astype(BF16)


def _mlstm_k_transposed(h3, w_t, layer, qw):
    bsz, t_all, d = h3.shape
    tk = _tile(t_all, 256, LANES)
    return pl.pallas_call(
        _kt_proj_kernel,
        grid=(bsz, t_all // tk),
        in_specs=[
            pl.BlockSpec((None, tk, d), lambda b, t: (b, t, 0)),
            pl.BlockSpec((None, qw, d), lambda b, t: (layer, 1, 0), pipeline_mode=pl.Buffered(1)),
        ],
        out_specs=pl.BlockSpec((None, qw, tk), lambda b, t: (b, 0, t)),
        out_shape=jax.ShapeDtypeStruct((bsz, qw, t_all), BF16),
        scratch_shapes=[pltpu.VMEM((qw, d), BF16)],
        compiler_params=_params("arbitrary", "arbitrary"),
        name="mlstm_k_transposed",
    )(h3, w_t)


def _mlstm_chunk(qb, kt, vx, a_row, cm_col, b_col, cx, m, reverse):
    L = qb.shape[0]
    dv = vx.shape[1] - LANES
    t_idx = lax.broadcasted_iota(jnp.int32, (L, L), 0)
    s_idx = lax.broadcasted_iota(jnp.int32, (L, L), 1)
    seen = s_idx >= t_idx if reverse else s_idx <= t_idx
    last = 0 if reverse else L - 1
    m_run = jnp.maximum(m, cm_col)
    s = _dot(qb, kt) * jnp.exp(jnp.where(seen, a_row - m_run, -jnp.inf))
    nd = jnp.exp(m - m_run) * _dot(qb, cx.astype(BF16)) + _dot(s.astype(BF16), vx)
    h = nd[:, :dv] / jnp.maximum(jnp.abs(nd[:, dv:dv + 1]), jnp.exp(-(b_col + m_run)))
    m_last = m_run[last:last + 1, :]
    kw = (kt.astype(F32) * jnp.exp(a_row - m_last)).astype(BF16)
    cx_new = jnp.exp(m - m_last) * cx + _dot(kw, vx)
    return h, cx_new, b_col[last:last + 1, :] + m_last


def _mlstm_streams(q_ref, kt_ref, v_ref, gcol_ref, grow_ref, state_refs, reverse, emit):
    bsz = q_ref.shape[0]
    dk = q_ref.shape[-1] // M_HEADS
    dv = v_ref.shape[-1] // M_HEADS
    ones = jnp.ones((q_ref.shape[1], LANES), BF16)

    @pl.when(pl.program_id(0) == 0)
    def _():
        for ref in state_refs:
            ref[...] = jnp.zeros_like(ref)

    for b in range(bsz):
        for hd in range(M_HEADS):
            cx_ref, m_ref = state_refs[2 * (b * M_HEADS + hd):2 * (b * M_HEADS + hd) + 2]
            st = (M_HEADS if reverse else 0) + hd
            vx = jnp.concatenate([v_ref[b, :, hd * dv:(hd + 1) * dv], ones], axis=1)
            h, cx_new, m_new = _mlstm_chunk(
                q_ref[b, :, hd * dk:(hd + 1) * dk], kt_ref[b, hd * dk:(hd + 1) * dk, :], vx,
                grow_ref[b, st:st + 1, :], gcol_ref[b, :, st:st + 1],
                gcol_ref[b, :, 2 * M_HEADS + st:2 * M_HEADS + st + 1],
                cx_ref[...], m_ref[...], reverse)
            cx_ref[...] = cx_new
            m_ref[...] = m_new
            emit(b, hd, h)


def _mlstm_fwd_kernel(q_ref, kt_ref, v_ref, gcol_ref, grow_ref, h_ref, *state_refs):
    dv = v_ref.shape[-1] // M_HEADS

    def emit(b, hd, h):
        h_ref[b, :, hd * dv:(hd + 1) * dv] = h

    _mlstm_streams(q_ref, kt_ref, v_ref, gcol_ref, grow_ref, state_refs, False, emit)


def _mlstm_bwd_kernel(q_ref, kt_ref, v_ref, gcol_ref, grow_ref, hf_ref, o_ref, z_ref, gm_ref,
                      out_ref, *state_refs):
    dv = v_ref.shape[-1] // M_HEADS

    def emit(b, hd, h):
        cols = slice(hd * dv, (hd + 1) * dv)
        hs = hf_ref[b, :, cols] + h
        hn = hs * lax.rsqrt(jnp.mean(hs * hs, axis=-1, keepdims=True) + EPS) * gm_ref[:, cols]
        gated = hn * jax.nn.sigmoid(o_ref[b, :, cols].astype(F32)) * _silu(z_ref[b, :, cols].astype(F32))
        out_ref[b, :, cols] = gated.astype(out_ref.dtype)

    _mlstm_streams(q_ref, kt_ref, v_ref, gcol_ref, grow_ref, state_refs, True, emit)


def _mlstm(p, kt, gcol, grow, g_mlstm, ctx_len):
    bsz, t_all, _ = p.shape
    mw = g_mlstm.shape[-1]
    qw = mw // 2
    nc = t_all // CHUNK
    n_ctx = ctx_len // CHUNK
    dk, dv = qw // M_HEADS, mw // M_HEADS

    def rev_chunk(c):
        return jnp.where(c < n_ctx, n_ctx - 1 - c, nc - 1 - (c - n_ctx))

    def specs(chunk_of):
        return [
            pl.BlockSpec((bsz, CHUNK, qw), lambda c: (0, chunk_of(c), 3 * mw // qw)),
            pl.BlockSpec((bsz, qw, CHUNK), lambda c: (0, 0, chunk_of(c))),
            pl.BlockSpec((bsz, CHUNK, mw), lambda c: (0, chunk_of(c), 0)),
            pl.BlockSpec((bsz, CHUNK, N_GATES), lambda c: (0, chunk_of(c), 0)),
            pl.BlockSpec((bsz, 2 * M_HEADS, CHUNK), lambda c: (0, 0, chunk_of(c))),
        ]

    scratch = [pltpu.VMEM((dk, dv + LANES), F32), pltpu.VMEM((1, 1), F32)] * (bsz * M_HEADS)
    h_fwd = pl.pallas_call(
        _mlstm_fwd_kernel,
        grid=(nc,),
        in_specs=specs(lambda c: c),
        out_specs=pl.BlockSpec((bsz, CHUNK, mw), lambda c: (0, c, 0)),
        out_shape=jax.ShapeDtypeStruct((bsz, t_all, mw), F32),
        scratch_shapes=scratch,
        compiler_params=_params("arbitrary"),
        name="mlstm_fwd",
    )(p, kt, p, gcol, grow)
    return pl.pallas_call(
        _mlstm_bwd_kernel,
        grid=(nc,),
        in_specs=specs(rev_chunk) + [
            pl.BlockSpec((bsz, CHUNK, mw), lambda c: (0, rev_chunk(c), 0)),
            pl.BlockSpec((bsz, CHUNK, mw), lambda c: (0, rev_chunk(c), 1)),
            pl.BlockSpec((bsz, CHUNK, mw), lambda c: (0, rev_chunk(c), 2)),
            pl.BlockSpec((1, mw), lambda c: (0, 0)),
        ],
        out_specs=pl.BlockSpec((bsz, CHUNK, mw), lambda c: (0, rev_chunk(c), 0)),
        out_shape=jax.ShapeDtypeStruct((bsz, t_all, mw), BF16),
        scratch_shapes=scratch,
        compiler_params=_params("arbitrary"),
        name="mlstm_bwd",
    )(p, kt, p, gcol, grow, h_fwd, p, p, g_mlstm.reshape(1, mw))


def _attn_kernel(q_ref, k_ref, v_ref, z_ref, o_ref, vx_ref, *stage_refs, ctx_len, key_chunk, update_ctx):
    tq = q_ref.shape[0]
    t_all = k_ref.shape[0]
    n_split = len(stage_refs) // 2
    s_refs, p_refs = stage_refs[:n_split], stage_refs[n_split:]
    heads_per_split = GQA_GROUP // n_split
    rows_per_split = heads_per_split * tq

    @pl.when(pl.program_id(2) == 0)
    def _():
        vx_ref[:, :HEAD_DIM] = v_ref[...]
        vx_ref[:, HEAD_DIM:] = jnp.ones((t_all, HEAD_DIM), BF16)

    def attend(n_keys):
        n_tiles = n_keys // LANES

        def scores(sp):
            q = jnp.concatenate([q_ref[:, g * HEAD_DIM:(g + 1) * HEAD_DIM]
                                 for g in range(sp * heads_per_split, (sp + 1) * heads_per_split)], axis=0)
            for lo in range(0, n_keys, key_chunk):
                hi = min(lo + key_chunk, n_keys)
                s_refs[sp][:, lo:hi] = _dot_nt(q, k_ref[lo:hi, :])

        def softmax(sp):
            s_ref, p_ref = s_refs[sp], p_refs[sp]
            for rb in range(rows_per_split // SOFTMAX_ROWS):
                rows = slice(rb * SOFTMAX_ROWS, (rb + 1) * SOFTMAX_ROWS)
                m_lanes = s_ref[rows, 0:LANES]
                for t in range(1, n_tiles):
                    m_lanes = jnp.maximum(m_lanes, s_ref[rows, t * LANES:(t + 1) * LANES])
                m_rows = jnp.broadcast_to(jnp.max(m_lanes, axis=-1, keepdims=True), (SOFTMAX_ROWS, LANES))
                for t in range(n_tiles):
                    cols = slice(t * LANES, (t + 1) * LANES)
                    p_ref[rows, cols] = jnp.exp2(s_ref[rows, cols] - m_rows).astype(BF16)

        def values(sp):
            ov = _dot(p_refs[sp][:, 0:n_keys], vx_ref[0:n_keys, :])
            o = ov[:, :HEAD_DIM] / ov[:, HEAD_DIM:HEAD_DIM + 1]
            for gl in range(heads_per_split):
                g = sp * heads_per_split + gl
                cols = slice(g * HEAD_DIM, (g + 1) * HEAD_DIM)
                gate = _silu(z_ref[:, cols].astype(F32))
                o_ref[:, cols] = (o[gl * tq:(gl + 1) * tq] * gate).astype(o_ref.dtype)

        scores(0)
        for sp in range(n_split):
            softmax(sp)
            if sp + 1 < n_split:
                scores(sp + 1)
            values(sp)

    @pl.when(pl.program_id(2) == 0)
    def _():
        if update_ctx:
            attend(ctx_len)
        else:
            o_ref[...] = jnp.zeros_like(o_ref)

    @pl.when(pl.program_id(2) > 0)
    def _():
        attend(t_all)


def _attention(p, mw, aw, ctx_len, update_ctx):
    bsz, t_all, _ = p.shape
    kvw = aw // GQA_GROUP
    kv_heads = kvw // HEAD_DIM
    gw = GQA_GROUP * HEAD_DIM
    tq = ctx_len
    assert (t_all - ctx_len) % tq == 0 and tq % SOFTMAX_ROWS == 0
    base = 3 * mw + mw // 2
    q_blk = base // gw
    k_blk = (base + aw) // HEAD_DIM
    v_blk = (base + aw + kvw) // HEAD_DIM
    z_blk = (base + aw + 2 * kvw) // gw
    n_split = 2
    kern = functools.partial(_attn_kernel, ctx_len=ctx_len, key_chunk=512, update_ctx=update_ctx)
    return pl.pallas_call(
        kern,
        grid=(bsz, kv_heads, t_all // tq),
        in_specs=[
            pl.BlockSpec((None, tq, gw), lambda b, h, i: (b, i, q_blk + h)),
            pl.BlockSpec((None, t_all, HEAD_DIM), lambda b, h, i: (b, 0, k_blk + h)),
            pl.BlockSpec((None, t_all, HEAD_DIM), lambda b, h, i: (b, 0, v_blk + h)),
            pl.BlockSpec((None, tq, gw), lambda b, h, i: (b, i, z_blk + h)),
        ],
        out_specs=pl.BlockSpec((None, tq, gw), lambda b, h, i: (b, i, h)),
        out_shape=jax.ShapeDtypeStruct((bsz, t_all, aw), BF16),
        scratch_shapes=([pltpu.VMEM((t_all, 2 * HEAD_DIM), BF16)]
                        + [pltpu.VMEM((GQA_GROUP * tq // n_split, t_all), F32)] * n_split
                        + [pltpu.VMEM((GQA_GROUP * tq // n_split, t_all), BF16)] * n_split),
        compiler_params=_params("arbitrary", "arbitrary", "arbitrary"),
        name="attention",
    )(p, p, p, p)


def _out_proj_kernel(am_ref, aa_ref, wm_ref, wa_ref, x_ref, gl_ref, gc_ref, o_ref, wmb_ref, wab_ref, *,
                     ctx_len, tiles_per_batch):
    tm = x_ref.shape[0]

    @pl.when(pl.program_id(1) == 0)
    def _():
        wmb_ref[...] = wm_ref[...].astype(BF16)
        wab_ref[...] = wa_ref[...].astype(BF16)

    row0 = (pl.program_id(1) % tiles_per_batch) * tm
    part = tm // PROJECTION_ROW_PARTS
    for p in range(PROJECTION_ROW_PARTS):
        rows = slice(p * part, (p + 1) * part)
        y = _dot(am_ref[rows, :], wmb_ref[...]) + _dot(aa_ref[rows, :], wab_ref[...])
        row = row0 + p * part + lax.broadcasted_iota(jnp.int32, y.shape, 0)
        gate = jnp.where(row < ctx_len, gc_ref[...], gl_ref[...])
        o_ref[rows, :] = x_ref[rows, :] + gate * y


def _out_projection(m_out, a_out, w_out, layer, x_all, mod_rows, ctx_len):
    bsz, t_all, d = x_all.shape
    mw = m_out.shape[-1]
    aw = a_out.shape[-1]
    assert mw == aw
    tm = _tile(t_all, 1152, 16 * PROJECTION_ROW_PARTS)
    tpb = t_all // tm
    tn = _tile(d, 512, LANES)
    m = bsz * t_all
    kern = functools.partial(_out_proj_kernel, ctx_len=ctx_len, tiles_per_batch=tpb)
    out = pl.pallas_call(
        kern,
        grid=(d // tn, m // tm),
        in_specs=[
            pl.BlockSpec((tm, mw), lambda j, i: (i, 0)),
            pl.BlockSpec((tm, aw), lambda j, i: (i, 0)),
            pl.BlockSpec((None, mw, tn), lambda j, i: (layer, 0, j)),
            pl.BlockSpec((None, aw, tn), lambda j, i: (layer, 1, j)),
            pl.BlockSpec((tm, tn), lambda j, i: (i, j)),
            pl.BlockSpec((None, None, 1, tn), lambda j, i: (i // tpb, 2, 0, j)),
            pl.BlockSpec((None, None, 1, tn), lambda j, i: (bsz, 2, 0, j)),
        ],
        out_specs=pl.BlockSpec((tm, tn), lambda j, i: (i, j)),
        out_shape=jax.ShapeDtypeStruct((m, d), F32),
        scratch_shapes=[pltpu.VMEM((mw, tn), BF16), pltpu.VMEM((aw, tn), BF16)],
        compiler_params=_params("arbitrary", "arbitrary"),
        name="out_projection",
    )(m_out.reshape(m, mw), a_out.reshape(m, aw), w_out, w_out, x_all.reshape(m, d), mod_rows, mod_rows)
    return out.reshape(bsz, t_all, d)


def _final_norm_kernel(x_ref, g_ref, o_ref):
    x = x_ref[...]
    o_ref[...] = x * lax.rsqrt(jnp.mean(x * x, axis=-1, keepdims=True) + EPS) * g_ref[...]


def _final_norm(x_all, g_final, ctx_len):
    bsz, t_all, d = x_all.shape
    seq = t_all - ctx_len
    tm = _tile(math.gcd(ctx_len, seq), 256, 8)
    off = ctx_len // tm
    return pl.pallas_call(
        _final_norm_kernel,
        grid=(bsz, seq // tm),
        in_specs=[
            pl.BlockSpec((None, tm, d), lambda b, t: (b, t + off, 0)),
            pl.BlockSpec((1, d), lambda b, t: (0, 0)),
        ],
        out_specs=pl.BlockSpec((None, tm, d), lambda b, t: (b, t, 0)),
        out_shape=jax.ShapeDtypeStruct((bsz, seq, d), F32),
        compiler_params=_params("arbitrary", "arbitrary"),
        name="final_norm",
    )(x_all, g_final.reshape(1, d))


def kernel(x, c, ctx, c_ctx, w_mod, b_mod, g_norm, w_in, b_gate, g_mlstm, g_q, g_k, w_out, g_final):
    bsz, seq, d = x.shape
    ctx_len = ctx.shape[1]
    depth = w_mod.shape[0]
    mw = g_mlstm.shape[-1]
    aw = w_out.shape[1] - mw
    gate_col = 4 * mw
    assert mw == aw and w_in.shape[-1] == gate_col + N_GATES + aw * 5 // 2
    assert ctx_len % CHUNK == 0 and seq % CHUNK == 0 and seq % GRID_W == 0

    w_t = jnp.swapaxes(w_in, 1, 2)
    cos_t, sin_t = _rope_tables(seq, ctx_len)

    c_rows = jnp.concatenate([c, c_ctx[None, :]], axis=0)
    n_rows = bsz + 1
    c_rows = jnp.pad(c_rows, ((0, -n_rows % 8), (0, 0)))
    mod = _modulation(c_rows, w_mod, b_mod)
    mod = mod[:, :n_rows].reshape(depth, n_rows, 3, 1, d)

    x_all = None
    for layer in range(depth):
        update_ctx = layer < depth - 1
        prologue_args = (mod[layer], g_norm[layer], w_t, layer, gate_col, b_gate[layer], ctx_len)
        if layer == 0:
            h, gcol, grow, x_all = _prologue((ctx, x), *prologue_args)
        else:
            h, gcol, grow = _prologue(x_all, *prologue_args)
        p = _in_projection(h, w_t, layer, cos_t, sin_t, g_q[layer], g_k[layer], mw, aw)
        kt = _mlstm_k_transposed(h, w_t, layer, mw // 2)
        m_out = _mlstm(p, kt, gcol, grow, g_mlstm[layer], ctx_len)
        a_out = _attention(p, mw, aw, ctx_len, update_ctx)
        x_all = _out_projection(m_out, a_out, w_out, layer, x_all, mod[layer], ctx_len)
    return _final_norm(x_all, g_final, ctx_len)
```

```python
import functools
import math

import jax
import jax.numpy as jnp
from jax import lax
from jax.experimental import pallas as pl
from jax.experimental.pallas import tpu as pltpu

CHUNK = 256
M_HEADS = 4
HEAD_DIM = 128
GQA_GROUP = 4
GRID_W = 64
ROPE_THETA = 10000.0
EPS = 1e-6
N_GATES = 4 * M_HEADS
LOG2_E = 1.4426950408889634
SOFTMAX_ROWS = 128
PROJECTION_ROW_PARTS = 4
IN_PROJECTION_ROW_PARTS = 2

LANES = 128
V7X_VMEM_LIMIT_BYTES = 56 * 1024 * 1024

F32 = jnp.float32
BF16 = jnp.bfloat16


def _params(*sem):
    return pltpu.CompilerParams(dimension_semantics=sem, vmem_limit_bytes=V7X_VMEM_LIMIT_BYTES)


def _tile(total, target, multiple):
    best = None
    for t in range(multiple, min(total, target) + 1, multiple):
        if total % t == 0:
            best = t
    assert best is not None, (total, target, multiple)
    return best


def _dot(a, b):
    return jnp.dot(a, b, preferred_element_type=F32)


def _dot_nt(a, b):
    return lax.dot_general(a, b, (((1,), (1,)), ((), ())), preferred_element_type=F32)


def _dot_tn(a, b):
    return lax.dot_general(a, b, (((0,), (0,)), ((), ())), preferred_element_type=F32)


def _silu(x):
    return x * jax.nn.sigmoid(x)


def _mod_kernel(c_ref, w_ref, b_ref, o_ref):
    @pl.when(pl.program_id(1) == 0)
    def _():
        o_ref[...] = jnp.broadcast_to(b_ref[...], o_ref.shape)

    o_ref[...] += _dot(_silu(c_ref[...]).astype(BF16), w_ref[...].astype(BF16))


def _modulation(c_rows, w_mod, b_mod):
    depth, d, n = w_mod.shape
    rows = c_rows.shape[0]
    tk = _tile(d, 256, LANES)
    c_chunks = c_rows.reshape(rows, d // tk, tk).swapaxes(0, 1)
    return pl.pallas_call(
        _mod_kernel,
        grid=(depth, d // tk),
        in_specs=[
            pl.BlockSpec((None, rows, tk), lambda l, k: (k, 0, 0)),
            pl.BlockSpec((None, tk, n), lambda l, k: (l, k, 0)),
            pl.BlockSpec((None, 1, n), lambda l, k: (l, 0, 0)),
        ],
        out_specs=pl.BlockSpec((None, rows, n), lambda l, k: (l, 0, 0)),
        out_shape=jax.ShapeDtypeStruct((depth, rows, n), F32),
        compiler_params=_params("arbitrary", "arbitrary"),
        name="modulation",
    )(c_chunks, w_mod, b_mod.reshape(depth, 1, n))


def _log_sigmoid(x):
    return jnp.minimum(x, 0.0) - jnp.log1p(jnp.exp(-jnp.abs(x)))


def _prologue_first_kernel(ctx_ref, lat_ref, g_ref, scale_ref, shift_ref, wg_ref, bg_ref,
                           h_ref, gcol_ref, grow_ref, xall_ref, *, n_ctx_tiles):
    @pl.when(pl.program_id(1) < n_ctx_tiles)
    def _():
        xall_ref[...] = ctx_ref[...]

    @pl.when(pl.program_id(1) >= n_ctx_tiles)
    def _():
        xall_ref[...] = lat_ref[...]

    _prologue_kernel(xall_ref, g_ref, scale_ref, shift_ref, wg_ref, bg_ref, h_ref, gcol_ref, grow_ref)


def _prologue_kernel(x_ref, g_ref, scale_ref, shift_ref, wg_ref, bg_ref, h_ref, gcol_ref, grow_ref):
    x = x_ref[...]
    y = x * lax.rsqrt(jnp.mean(x * x, axis=-1, keepdims=True) + EPS)
    h = (y * g_ref[...]) * (1.0 + scale_ref[...]) + shift_ref[...]
    hb = h.astype(BF16)
    h_ref[...] = hb
    _scan_gate_terms(hb, wg_ref[...], bg_ref[...], gcol_ref, grow_ref)


def _split3(x):
    hi = x.astype(BF16)
    rest = x - hi.astype(F32)
    mid = rest.astype(BF16)
    return hi, mid, (rest - mid.astype(F32)).astype(BF16)


def _scan_gate_terms(hb, wg, bg_row, gcol_ref, grow_ref):
    H = M_HEADS
    tm = hb.shape[0]
    lane = lax.broadcasted_iota(jnp.int32, (tm, LANES), 1)
    pre = _dot_nt(hb, wg.astype(BF16)) + bg_row
    gates = jnp.where((lane & H) != 0, _log_sigmoid(pre), pre)
    t_idx = lax.broadcasted_iota(jnp.int32, (tm, tm), 0)
    s_idx = lax.broadcasted_iota(jnp.int32, (tm, tm), 1)
    same_chunk = (t_idx // CHUNK) == (s_idx // CHUNK)
    at_or_before = jnp.logical_and(same_chunk, s_idx <= t_idx)
    at_or_after = jnp.logical_and(same_chunk, s_idx >= t_idx)
    lower = jnp.where(at_or_before, 1.0, 0.0).astype(BF16)
    upper = jnp.where(at_or_after, 1.0, 0.0).astype(BF16)
    parts = _split3(gates)
    sum_before = sum(_dot(lower, p) for p in parts)
    sum_after = sum(_dot(upper, p) for p in parts)
    b_col = jnp.where(lane >= 2 * H, sum_after, sum_before)
    gates_row, b_row = gates.T, b_col.T
    a_row = jnp.concatenate([gates_row[0:H] - b_row[H:2 * H],
                             gates_row[2 * H:3 * H] - b_row[3 * H:4 * H]], axis=0)
    cm_cols = []
    for st in range(2 * H):
        seen = at_or_after if st >= H else at_or_before
        cm_cols.append(jnp.max(jnp.where(seen, a_row[st:st + 1, :], -jnp.inf), axis=1, keepdims=True))
    gcol_ref[...] = jnp.concatenate(cm_cols + [b_col[:, H:2 * H], b_col[:, 3 * H:4 * H]], axis=1)
    grow_ref[...] = a_row


def _prologue(tokens, mod_rows, g_norm, w_t, layer, gate_row, b_gate, ctx_len):
    first = isinstance(tokens, tuple)
    if first:
        ctx, lat = tokens
        bsz, seq, d = lat.shape
        t_all = ctx_len + seq
    else:
        bsz, t_all, d = tokens.shape
    tm = _tile(math.gcd(ctx_len, t_all - ctx_len), 256, LANES)
    n_ctx_tiles = ctx_len // tm
    assert gate_row % LANES == 0 and gate_row + LANES <= w_t.shape[1]

    def mod_row(b, t):
        return jnp.where(t < n_ctx_tiles, bsz, b)

    tile_spec = pl.BlockSpec((None, tm, d), lambda b, t: (b, t, 0))
    if first:
        kern = functools.partial(_prologue_first_kernel, n_ctx_tiles=n_ctx_tiles)
        token_specs = [
            pl.BlockSpec((None, tm, d), lambda b, t: (b, jnp.minimum(t, n_ctx_tiles - 1), 0)),
            pl.BlockSpec((None, tm, d), lambda b, t: (b, jnp.maximum(t - n_ctx_tiles, 0), 0)),
        ]
        token_args = [ctx, lat]
    else:
        kern, token_specs, token_args = _prologue_kernel, [tile_spec], [tokens]
    out_specs = [tile_spec,
                 pl.BlockSpec((None, tm, N_GATES), lambda b, t: (b, t, 0)),
                 pl.BlockSpec((None, 2 * M_HEADS, tm), lambda b, t: (b, 0, t))]
    out_shape = [jax.ShapeDtypeStruct((bsz, t_all, d), BF16),
                 jax.ShapeDtypeStruct((bsz, t_all, N_GATES), F32),
                 jax.ShapeDtypeStruct((bsz, 2 * M_HEADS, t_all), F32)]
    if first:
        out_specs.append(tile_spec)
        out_shape.append(jax.ShapeDtypeStruct((bsz, t_all, d), F32))
    return pl.pallas_call(
        kern,
        grid=(bsz, t_all // tm),
        in_specs=token_specs + [
            pl.BlockSpec((1, d), lambda b, t: (0, 0)),
            pl.BlockSpec((None, None, 1, d), lambda b, t: (mod_row(b, t), 1, 0, 0)),
            pl.BlockSpec((None, None, 1, d), lambda b, t: (mod_row(b, t), 0, 0, 0)),
            pl.BlockSpec((None, LANES, d), lambda b, t: (layer, gate_row // LANES, 0)),
            pl.BlockSpec((1, LANES), lambda b, t: (0, 0)),
        ],
        out_specs=out_specs,
        out_shape=out_shape,
        compiler_params=_params("arbitrary", "arbitrary"),
        name="prologue",
    )(*token_args, g_norm.reshape(1, d), mod_rows, mod_rows, w_t,
      jnp.pad(b_gate, (0, LANES - N_GATES)).reshape(1, LANES))


def _rope_tables(seq, ctx_len):
    axis_dim = HEAD_DIM // 2
    rows = seq // GRID_W
    row_ids = jnp.repeat(jnp.arange(rows), GRID_W).astype(F32)
    col_ids = jnp.tile(jnp.arange(GRID_W), rows).astype(F32)
    inv = ROPE_THETA ** (-jnp.arange(0, axis_dim, 2, dtype=F32) / axis_dim)
    ang_r, ang_c = row_ids[:, None] * inv, col_ids[:, None] * inv
    cos_t = jnp.concatenate([jnp.cos(ang_r)] * 2 + [jnp.cos(ang_c)] * 2, axis=-1)
    sin_t = jnp.concatenate([-jnp.sin(ang_r), jnp.sin(ang_r), -jnp.sin(ang_c), jnp.sin(ang_c)], axis=-1)
    cos_t = jnp.concatenate([jnp.ones((ctx_len, HEAD_DIM), F32), cos_t], axis=0)
    sin_t = jnp.concatenate([jnp.zeros((ctx_len, HEAD_DIM), F32), sin_t], axis=0)
    return cos_t, sin_t


def _skip_mlstm_k(j, mq_hi, mv_lo):
    return j + jnp.where(j >= mq_hi, mv_lo - mq_hi, 0)


def _head_norm_rope(x, g, cos_t, sin_t):
    quarter = HEAD_DIM // 4
    lane = lax.broadcasted_iota(jnp.int32, x.shape, 1)
    first_half = (lane % (2 * quarter)) < quarter
    xn = x * lax.rsqrt(jnp.mean(x * x, axis=-1, keepdims=True) + EPS) * g
    swapped = jnp.where(first_half, pltpu.roll(xn, HEAD_DIM - quarter, axis=1), pltpu.roll(xn, quarter, axis=1))
    return xn * cos_t + swapped * sin_t


def _in_proj_kernel(a_ref, w_ref, cos_ref, sin_ref, gq_ref, gk_ref, p_ref, wb_ref, *,
                    mq_hi, mq_scale, mv_lo, q_lo, kv_tile, k_heads):
    j = _skip_mlstm_k(pl.program_id(0), mq_hi, mv_lo)

    @pl.when(pl.program_id(1) == 0)
    def _():
        wb_ref[...] = w_ref[...].astype(BF16)

    def row_parts():
        part = a_ref.shape[0] // IN_PROJECTION_ROW_PARTS
        for p in range(IN_PROJECTION_ROW_PARTS):
            rows = slice(p * part, (p + 1) * part)
            yield rows, _dot_nt(a_ref[rows, :], wb_ref[...])

    def rotate_heads(rows, y, heads, gain, post_scale):
        for hd in heads:
            cols = slice(hd * HEAD_DIM, (hd + 1) * HEAD_DIM)
            rotated = _head_norm_rope(y[:, cols], gain, cos_ref[rows, :], sin_ref[rows, :])
            p_ref[rows, cols] = (rotated * post_scale).astype(BF16)

    n_heads = p_ref.shape[-1] // HEAD_DIM
    is_q = jnp.logical_and(j >= q_lo, j < kv_tile)

    @pl.when(jnp.logical_not(jnp.logical_or(is_q, j == kv_tile)))
    def _():
        scale = jnp.where(j < mq_hi, mq_scale, 1.0)
        for rows, y in row_parts():
            p_ref[rows, :] = (y * scale).astype(BF16)

    @pl.when(is_q)
    def _():
        for rows, y in row_parts():
            rotate_heads(rows, y, range(n_heads), gq_ref[...], HEAD_DIM ** -0.5 * LOG2_E)

    @pl.when(j == kv_tile)
    def _():
        for rows, y in row_parts():
            rotate_heads(rows, y, range(k_heads), gk_ref[...], 1.0)
            p_ref[rows, k_heads * HEAD_DIM:] = y[:, k_heads * HEAD_DIM:].astype(BF16)


def _in_projection(h3, w_t, layer, cos_t, sin_t, g_q, g_k, mw, aw):
    bsz, t_all, d = h3.shape
    m = bsz * t_all
    kvw = aw // GQA_GROUP
    qw = mw // 2
    tn = 2 * kvw
    assert qw % tn == 0 and aw % tn == 0
    tm = _tile(t_all, 576, 16 * IN_PROJECTION_ROW_PARTS)
    tiles_per_batch = t_all // tm
    mq_hi = qw // tn
    mv_lo = 2 * mq_hi
    q_lo = 4 * mw // tn
    kv_tile = q_lo + aw // tn
    n_j = kv_tile + 1 + aw // tn

    def tile_of(j):
        return _skip_mlstm_k(j, mq_hi, mv_lo)

    def w_row(j):
        return pl.multiple_of(j * tn + jnp.where(j >= q_lo, N_GATES, 0), N_GATES)

    def out_col(j):
        return jnp.where(j < mq_hi, j + (q_lo - mv_lo), jnp.where(j < q_lo, j - mv_lo, j - mq_hi))

    kern = functools.partial(_in_proj_kernel, mq_hi=mq_hi, mq_scale=(qw // M_HEADS) ** -0.5, mv_lo=mv_lo,
                             q_lo=q_lo, kv_tile=kv_tile, k_heads=kvw // HEAD_DIM)
    p = pl.pallas_call(
        kern,
        grid=(n_j - (mv_lo - mq_hi), m // tm),
        in_specs=[
            pl.BlockSpec((tm, d), lambda j, i: (i, 0)),
            pl.BlockSpec((None, pl.Element(tn), pl.Element(d)), lambda j, i: (layer, w_row(tile_of(j)), 0)),
            pl.BlockSpec((tm, HEAD_DIM), lambda j, i: (i % tiles_per_batch, 0)),
            pl.BlockSpec((tm, HEAD_DIM), lambda j, i: (i % tiles_per_batch, 0)),
            pl.BlockSpec((1, HEAD_DIM), lambda j, i: (0, 0)),
            pl.BlockSpec((1, HEAD_DIM), lambda j, i: (0, 0)),
        ],
        out_specs=pl.BlockSpec((tm, tn), lambda j, i: (i, out_col(tile_of(j)))),
        out_shape=jax.ShapeDtypeStruct((m, 3 * mw + qw + 2 * aw + 2 * kvw), BF16),
        scratch_shapes=[pltpu.VMEM((tn, d), BF16)],
        compiler_params=_params("arbitrary", "arbitrary"),
        name="in_projection",
    )(h3.reshape(m, d), w_t, cos_t, sin_t, g_q.reshape(1, HEAD_DIM), g_k.reshape(1, HEAD_DIM))
    return p.reshape(bsz, t_all, -1)


def _kt_proj_kernel(h_ref, w_ref, o_ref, wb_ref):
    @pl.when(jnp.logical_and(pl.program_id(0) == 0, pl.program_id(1) == 0))
    def _():
        wb_ref[...] = w_ref[...].astype(BF16)

    o_ref[...] = _dot_nt(wb_ref[...], h_ref[...]).astype(BF16)


def _mlstm_k_transposed(h3, w_t, layer, qw):
    bsz, t_all, d = h3.shape
    tk = _tile(t_all, 256, LANES)
    return pl.pallas_call(
        _kt_proj_kernel,
        grid=(bsz, t_all // tk),
        in_specs=[
            pl.BlockSpec((None, tk, d), lambda b, t: (b, t, 0)),
            pl.BlockSpec((None, qw, d), lambda b, t: (layer, 1, 0), pipeline_mode=pl.Buffered(1)),
        ],
        out_specs=pl.BlockSpec((None, qw, tk), lambda b, t: (b, 0, t)),
        out_shape=jax.ShapeDtypeStruct((bsz, qw, t_all), BF16),
        scratch_shapes=[pltpu.VMEM((qw, d), BF16)],
        compiler_params=_params("arbitrary", "arbitrary"),
        name="mlstm_k_transposed",
    )(h3, w_t)


def _mlstm_chunk(qb, kt, vx, a_row, cm_col, b_col, cx, m, reverse):
    L = qb.shape[0]
    dv = vx.shape[1] - LANES
    t_idx = lax.broadcasted_iota(jnp.int32, (L, L), 0)
    s_idx = lax.broadcasted_iota(jnp.int32, (L, L), 1)
    seen = s_idx >= t_idx if reverse else s_idx <= t_idx
    last = 0 if reverse else L - 1
    m_run = jnp.maximum(m, cm_col)
    s = _dot(qb, kt) * jnp.exp(jnp.where(seen, a_row - m_run, -jnp.inf))
    nd = jnp.exp(m - m_run) * _dot(qb, cx.astype(BF16)) + _dot(s.astype(BF16), vx)
    h = nd[:, :dv] / jnp.maximum(jnp.abs(nd[:, dv:dv + 1]), jnp.exp(-(b_col + m_run)))
    m_last = m_run[last:last + 1, :]
    kw = (kt.astype(F32) * jnp.exp(a_row - m_last)).astype(BF16)
    cx_new = jnp.exp(m - m_last) * cx + _dot(kw, vx)
    return h, cx_new, b_col[last:last + 1, :] + m_last


def _mlstm_streams(q_ref, kt_ref, v_ref, gcol_ref, grow_ref, state_refs, reverse, emit):
    bsz = q_ref.shape[0]
    dk = q_ref.shape[-1] // M_HEADS
    dv = v_ref.shape[-1] // M_HEADS
    ones = jnp.ones((q_ref.shape[1], LANES), BF16)

    @pl.when(pl.program_id(0) == 0)
    def _():
        for ref in state_refs:
            ref[...] = jnp.zeros_like(ref)

    for b in range(bsz):
        for hd in range(M_HEADS):
            cx_ref, m_ref = state_refs[2 * (b * M_HEADS + hd):2 * (b * M_HEADS + hd) + 2]
            st = (M_HEADS if reverse else 0) + hd
            vx = jnp.concatenate([v_ref[b, :, hd * dv:(hd + 1) * dv], ones], axis=1)
            h, cx_new, m_new = _mlstm_chunk(
                q_ref[b, :, hd * dk:(hd + 1) * dk], kt_ref[b, hd * dk:(hd + 1) * dk, :], vx,
                grow_ref[b, st:st + 1, :], gcol_ref[b, :, st:st + 1],
                gcol_ref[b, :, 2 * M_HEADS + st:2 * M_HEADS + st + 1],
                cx_ref[...], m_ref[...], reverse)
            cx_ref[...] = cx_new
            m_ref[...] = m_new
            emit(b, hd, h)


def _mlstm_fwd_kernel(q_ref, kt_ref, v_ref, gcol_ref, grow_ref, h_ref, *state_refs):
    dv = v_ref.shape[-1] // M_HEADS

    def emit(b, hd, h):
        h_ref[b, :, hd * dv:(hd + 1) * dv] = h

    _mlstm_streams(q_ref, kt_ref, v_ref, gcol_ref, grow_ref, state_refs, False, emit)


def _mlstm_bwd_kernel(q_ref, kt_ref, v_ref, gcol_ref, grow_ref, hf_ref, o_ref, z_ref, gm_ref,
                      out_ref, *state_refs):
    dv = v_ref.shape[-1] // M_HEADS

    def emit(b, hd, h):
        cols = slice(hd * dv, (hd + 1) * dv)
        hs = hf_ref[b, :, cols] + h
        hn = hs * lax.rsqrt(jnp.mean(hs * hs, axis=-1, keepdims=True) + EPS) * gm_ref[:, cols]
        gated = hn * jax.nn.sigmoid(o_ref[b, :, cols].astype(F32)) * _silu(z_ref[b, :, cols].astype(F32))
        out_ref[b, :, cols] = gated.astype(out_ref.dtype)

    _mlstm_streams(q_ref, kt_ref, v_ref, gcol_ref, grow_ref, state_refs, True, emit)


def _mlstm(p, kt, gcol, grow, g_mlstm, ctx_len):
    bsz, t_all, _ = p.shape
    mw = g_mlstm.shape[-1]
    qw = mw // 2
    nc = t_all // CHUNK
    n_ctx = ctx_len // CHUNK
    dk, dv = qw // M_HEADS, mw // M_HEADS

    def rev_chunk(c):
        return jnp.where(c < n_ctx, n_ctx - 1 - c, nc - 1 - (c - n_ctx))

    def specs(chunk_of):
        return [
            pl.BlockSpec((bsz, CHUNK, qw), lambda c: (0, chunk_of(c), 3 * mw // qw)),
            pl.BlockSpec((bsz, qw, CHUNK), lambda c: (0, 0, chunk_of(c))),
            pl.BlockSpec((bsz, CHUNK, mw), lambda c: (0, chunk_of(c), 0)),
            pl.BlockSpec((bsz, CHUNK, N_GATES), lambda c: (0, chunk_of(c), 0)),
            pl.BlockSpec((bsz, 2 * M_HEADS, CHUNK), lambda c: (0, 0, chunk_of(c))),
        ]

    scratch = [pltpu.VMEM((dk, dv + LANES), F32), pltpu.VMEM((1, 1), F32)] * (bsz * M_HEADS)
    h_fwd = pl.pallas_call(
        _mlstm_fwd_kernel,
        grid=(nc,),
        in_specs=specs(lambda c: c),
        out_specs=pl.BlockSpec((bsz, CHUNK, mw), lambda c: (0, c, 0)),
        out_shape=jax.ShapeDtypeStruct((bsz, t_all, mw), F32),
        scratch_shapes=scratch,
        compiler_params=_params("arbitrary"),
        name="mlstm_fwd",
    )(p, kt, p, gcol, grow)
    return pl.pallas_call(
        _mlstm_bwd_kernel,
        grid=(nc,),
        in_specs=specs(rev_chunk) + [
            pl.BlockSpec((bsz, CHUNK, mw), lambda c: (0, rev_chunk(c), 0)),
            pl.BlockSpec((bsz, CHUNK, mw), lambda c: (0, rev_chunk(c), 1)),
            pl.BlockSpec((bsz, CHUNK, mw), lambda c: (0, rev_chunk(c), 2)),
            pl.BlockSpec((1, mw), lambda c: (0, 0)),
        ],
        out_specs=pl.BlockSpec((bsz, CHUNK, mw), lambda c: (0, rev_chunk(c), 0)),
        out_shape=jax.ShapeDtypeStruct((bsz, t_all, mw), BF16),
        scratch_shapes=scratch,
        compiler_params=_params("arbitrary"),
        name="mlstm_bwd",
    )(p, kt, p, gcol, grow, h_fwd, p, p, g_mlstm.reshape(1, mw))


def _attn_kernel(q_ref, k_ref, v_ref, z_ref, o_ref, vx_ref, *stage_refs, ctx_len, key_chunk, update_ctx):
    tq = q_ref.shape[0]
    t_all = k_ref.shape[0]
    n_split = len(stage_refs) // 2
    s_refs, p_refs = stage_refs[:n_split], stage_refs[n_split:]
    heads_per_split = GQA_GROUP // n_split
    rows_per_split = heads_per_split * tq

    @pl.when(pl.program_id(2) == 0)
    def _():
        vx_ref[:, :HEAD_DIM] = v_ref[...]
        vx_ref[:, HEAD_DIM:] = jnp.ones((t_all, HEAD_DIM), BF16)

    def attend(n_keys):
        n_tiles = n_keys // LANES

        def scores(sp):
            q = jnp.concatenate([q_ref[:, g * HEAD_DIM:(g + 1) * HEAD_DIM]
                                 for g in range(sp * heads_per_split, (sp + 1) * heads_per_split)], axis=0)
            for lo in range(0, n_keys, key_chunk):
                hi = min(lo + key_chunk, n_keys)
                s_refs[sp][:, lo:hi] = _dot_nt(q, k_ref[lo:hi, :])

        def softmax(sp):
            s_ref, p_ref = s_refs[sp], p_refs[sp]
            for rb in range(rows_per_split // SOFTMAX_ROWS):
                rows = slice(rb * SOFTMAX_ROWS, (rb + 1) * SOFTMAX_ROWS)
                m_lanes = s_ref[rows, 0:LANES]
                for t in range(1, n_tiles):
                    m_lanes = jnp.maximum(m_lanes, s_ref[rows, t * LANES:(t + 1) * LANES])
                m_rows = jnp.broadcast_to(jnp.max(m_lanes, axis=-1, keepdims=True), (SOFTMAX_ROWS, LANES))
                for t in range(n_tiles):
                    cols = slice(t * LANES, (t + 1) * LANES)
                    p_ref[rows, cols] = jnp.exp2(s_ref[rows, cols] - m_rows).astype(BF16)

        def values(sp):
            ov = _dot(p_refs[sp][:, 0:n_keys], vx_ref[0:n_keys, :])
            o = ov[:, :HEAD_DIM] / ov[:, HEAD_DIM:HEAD_DIM + 1]
            for gl in range(heads_per_split):
                g = sp * heads_per_split + gl
                cols = slice(g * HEAD_DIM, (g + 1) * HEAD_DIM)
                gate = _silu(z_ref[:, cols].astype(F32))
                o_ref[:, cols] = (o[gl * tq:(gl + 1) * tq] * gate).astype(o_ref.dtype)

        scores(0)
        for sp in range(n_split):
            softmax(sp)
            if sp + 1 < n_split:
                scores(sp + 1)
            values(sp)

    @pl.when(pl.program_id(2) == 0)
    def _():
        if update_ctx:
            attend(ctx_len)
        else:
            o_ref[...] = jnp.zeros_like(o_ref)

    @pl.when(pl.program_id(2) > 0)
    def _():
        attend(t_all)


def _attention(p, mw, aw, ctx_len, update_ctx):
    bsz, t_all, _ = p.shape
    kvw = aw // GQA_GROUP
    kv_heads = kvw // HEAD_DIM
    gw = GQA_GROUP * HEAD_DIM
    tq = ctx_len
    assert (t_all - ctx_len) % tq == 0 and tq % SOFTMAX_ROWS == 0
    base = 3 * mw + mw // 2
    q_blk = base // gw
    k_blk = (base + aw) // HEAD_DIM
    v_blk = (base + aw + kvw) // HEAD_DIM
    z_blk = (base + aw + 2 * kvw) // gw
    n_split = 2
    kern = functools.partial(_attn_kernel, ctx_len=ctx_len, key_chunk=512, update_ctx=update_ctx)
    return pl.pallas_call(
        kern,
        grid=(bsz, kv_heads, t_all // tq),
        in_specs=[
            pl.BlockSpec((None, tq, gw), lambda b, h, i: (b, i, q_blk + h)),
            pl.BlockSpec((None, t_all, HEAD_DIM), lambda b, h, i: (b, 0, k_blk + h)),
            pl.BlockSpec((None, t_all, HEAD_DIM), lambda b, h, i: (b, 0, v_blk + h)),
            pl.BlockSpec((None, tq, gw), lambda b, h, i: (b, i, z_blk + h)),
        ],
        out_specs=pl.BlockSpec((None, tq, gw), lambda b, h, i: (b, i, h)),
        out_shape=jax.ShapeDtypeStruct((bsz, t_all, aw), BF16),
        scratch_shapes=([pltpu.VMEM((t_all, 2 * HEAD_DIM), BF16)]
                        + [pltpu.VMEM((GQA_GROUP * tq // n_split, t_all), F32)] * n_split
                        + [pltpu.VMEM((GQA_GROUP * tq // n_split, t_all), BF16)] * n_split),
        compiler_params=_params("arbitrary", "arbitrary", "arbitrary"),
        name="attention",
    )(p, p, p, p)


def _out_proj_kernel(am_ref, aa_ref, wm_ref, wa_ref, x_ref, gl_ref, gc_ref, o_ref, wmb_ref, wab_ref, *,
                     ctx_len, tiles_per_batch):
    tm = x_ref.shape[0]

    @pl.when(pl.program_id(1) == 0)
    def _():
        wmb_ref[...] = wm_ref[...].astype(BF16)
        wab_ref[...] = wa_ref[...].astype(BF16)

    row0 = (pl.program_id(1) % tiles_per_batch) * tm
    part = tm // PROJECTION_ROW_PARTS
    for p in range(PROJECTION_ROW_PARTS):
        rows = slice(p * part, (p + 1) * part)
        y = _dot(am_ref[rows, :], wmb_ref[...]) + _dot(aa_ref[rows, :], wab_ref[...])
        row = row0 + p * part + lax.broadcasted_iota(jnp.int32, y.shape, 0)
        gate = jnp.where(row < ctx_len, gc_ref[...], gl_ref[...])
        o_ref[rows, :] = x_ref[rows, :] + gate * y


def _out_projection(m_out, a_out, w_out, layer, x_all, mod_rows, ctx_len):
    bsz, t_all, d = x_all.shape
    mw = m_out.shape[-1]
    aw = a_out.shape[-1]
    assert mw == aw
    tm = _tile(t_all, 1152, 16 * PROJECTION_ROW_PARTS)
    tpb = t_all // tm
    tn = _tile(d, 512, LANES)
    m = bsz * t_all
    kern = functools.partial(_out_proj_kernel, ctx_len=ctx_len, tiles_per_batch=tpb)
    out = pl.pallas_call(
        kern,
        grid=(d // tn, m // tm),
        in_specs=[
            pl.BlockSpec((tm, mw), lambda j, i: (i, 0)),
            pl.BlockSpec((tm, aw), lambda j, i: (i, 0)),
            pl.BlockSpec((None, mw, tn), lambda j, i: (layer, 0, j)),
            pl.BlockSpec((None, aw, tn), lambda j, i: (layer, 1, j)),
            pl.BlockSpec((tm, tn), lambda j, i: (i, j)),
            pl.BlockSpec((None, None, 1, tn), lambda j, i: (i // tpb, 2, 0, j)),
            pl.BlockSpec((None, None, 1, tn), lambda j, i: (bsz, 2, 0, j)),
        ],
        out_specs=pl.BlockSpec((tm, tn), lambda j, i: (i, j)),
        out_shape=jax.ShapeDtypeStruct((m, d), F32),
        scratch_shapes=[pltpu.VMEM((mw, tn), BF16), pltpu.VMEM((aw, tn), BF16)],
        compiler_params=_params("arbitrary", "arbitrary"),
        name="out_projection",
    )(m_out.reshape(m, mw), a_out.reshape(m, aw), w_out, w_out, x_all.reshape(m, d), mod_rows, mod_rows)
    return out.reshape(bsz, t_all, d)


def _final_norm_kernel(x_ref, g_ref, o_ref):
    x = x_ref[...]
    o_ref[...] = x * lax.rsqrt(jnp.mean(x * x, axis=-1, keepdims=True) + EPS) * g_ref[...]


def _final_norm(x_all, g_final, ctx_len):
    bsz, t_all, d = x_all.shape
    seq = t_all - ctx_len
    tm = _tile(math.gcd(ctx_len, seq), 256, 8)
    off = ctx_len // tm
    return pl.pallas_call(
        _final_norm_kernel,
        grid=(bsz, seq // tm),
        in_specs=[
            pl.BlockSpec((None, tm, d), lambda b, t: (b, t + off, 0)),
            pl.BlockSpec((1, d), lambda b, t: (0, 0)),
        ],
        out_specs=pl.BlockSpec((None, tm, d), lambda b, t: (b, t, 0)),
        out_shape=jax.ShapeDtypeStruct((bsz, seq, d), F32),
        compiler_params=_params("arbitrary", "arbitrary"),
        name="final_norm",
    )(x_all, g_final.reshape(1, d))


def kernel(x, c, ctx, c_ctx, w_mod, b_mod, g_norm, w_in, b_gate, g_mlstm, g_q, g_k, w_out, g_final):
    bsz, seq, d = x.shape
    ctx_len = ctx.shape[1]
    depth = w_mod.shape[0]
    mw = g_mlstm.shape[-1]
    aw = w_out.shape[1] - mw
    gate_col = 4 * mw
    assert mw == aw and w_in.shape[-1] == gate_col + N_GATES + aw * 5 // 2
    assert ctx_len % CHUNK == 0 and seq % CHUNK == 0 and seq % GRID_W == 0

    w_t = jnp.swapaxes(w_in, 1, 2)
    cos_t, sin_t = _rope_tables(seq, ctx_len)

    c_rows = jnp.concatenate([c, c_ctx[None, :]], axis=0)
    n_rows = bsz + 1
    c_rows = jnp.pad(c_rows, ((0, -n_rows % 8), (0, 0)))
    mod = _modulation(c_rows, w_mod, b_mod)
    mod = mod[:, :n_rows].reshape(depth, n_rows, 3, 1, d)

    x_all = None
    for layer in range(depth):
        update_ctx = layer < depth - 1
        prologue_args = (mod[layer], g_norm[layer], w_t, layer, gate_col, b_gate[layer], ctx_len)
        if layer == 0:
            h, gcol, grow, x_all = _prologue((ctx, x), *prologue_args)
        else:
            h, gcol, grow = _prologue(x_all, *prologue_args)
        p = _in_projection(h, w_t, layer, cos_t, sin_t, g_q[layer], g_k[layer], mw, aw)
        kt = _mlstm_k_transposed(h, w_t, layer, mw // 2)
        m_out = _mlstm(p, kt, gcol, grow, g_mlstm[layer], ctx_len)
        a_out = _attention(p, mw, aw, ctx_len, update_ctx)
        x_all = _out_projection(m_out, a_out, w_out, layer, x_all, mod[layer], ctx_len)
    return _final_norm(x_all, g_final, ctx_len)
```

```python
import functools
import math

import jax
import jax.numpy as jnp
from jax import lax
from jax.experimental import pallas as pl
from jax.experimental.pallas import tpu as pltpu

CHUNK = 256
M_HEADS = 4
HEAD_DIM = 128
GQA_GROUP = 4
GRID_W = 64
ROPE_THETA = 10000.0
EPS = 1e-6
N_GATES = 4 * M_HEADS
LOG2_E = 1.4426950408889634
SOFTMAX_ROWS = 128
PROJECTION_ROW_PARTS = 4
IN_PROJECTION_ROW_PARTS = 2

LANES = 128
V7X_VMEM_LIMIT_BYTES = 56 * 1024 * 1024

F32 = jnp.float32
BF16 = jnp.bfloat16


def _params(*sem):
    return pltpu.CompilerParams(dimension_semantics=sem, vmem_limit_bytes=V7X_VMEM_LIMIT_BYTES)


def _tile(total, target, multiple):
    best = None
    for t in range(multiple, min(total, target) + 1, multiple):
        if total % t == 0:
            best = t
    assert best is not None, (total, target, multiple)
    return best


def _dot(a, b):
    return jnp.dot(a, b, preferred_element_type=F32)


def _dot_nt(a, b):
    return lax.dot_general(a, b, (((1,), (1,)), ((), ())), preferred_element_type=F32)


def _dot_tn(a, b):
    return lax.dot_general(a, b, (((0,), (0,)), ((), ())), preferred_element_type=F32)


def _silu(x):
    return x * jax.nn.sigmoid(x)


def _mod_kernel(c_ref, w_ref, b_ref, o_ref):
    @pl.when(pl.program_id(1) == 0)
    def _():
        o_ref[...] = jnp.broadcast_to(b_ref[...], o_ref.shape)

    o_ref[...] += _dot(_silu(c_ref[...]).astype(BF16), w_ref[...].astype(BF16))


def _modulation(c_rows, w_mod, b_mod):
    depth, d, n = w_mod.shape
    rows = c_rows.shape[0]
    tk = _tile(d, 256, LANES)
    c_chunks = c_rows.reshape(rows, d // tk, tk).swapaxes(0, 1)
    return pl.pallas_call(
        _mod_kernel,
        grid=(depth, d // tk),
        in_specs=[
            pl.BlockSpec((None, rows, tk), lambda l, k: (k, 0, 0)),
            pl.BlockSpec((None, tk, n), lambda l, k: (l, k, 0)),
            pl.BlockSpec((None, 1, n), lambda l, k: (l, 0, 0)),
        ],
        out_specs=pl.BlockSpec((None, rows, n), lambda l, k: (l, 0, 0)),
        out_shape=jax.ShapeDtypeStruct((depth, rows, n), F32),
        compiler_params=_params("arbitrary", "arbitrary"),
        name="modulation",
    )(c_chunks, w_mod, b_mod.reshape(depth, 1, n))


def _log_sigmoid(x):
    return jnp.minimum(x, 0.0) - jnp.log1p(jnp.exp(-jnp.abs(x)))


def _prologue_first_kernel(ctx_ref, lat_ref, g_ref, scale_ref, shift_ref, wg_ref, bg_ref, wk_ref,
                           h_ref, gcol_ref, grow_ref, kt_ref, xall_ref, *, n_ctx_tiles):
    @pl.when(pl.program_id(1) < n_ctx_tiles)
    def _():
        xall_ref[...] = ctx_ref[...]

    @pl.when(pl.program_id(1) >= n_ctx_tiles)
    def _():
        xall_ref[...] = lat_ref[...]

    _prologue_kernel(xall_ref, g_ref, scale_ref, shift_ref, wg_ref, bg_ref, wk_ref, h_ref, gcol_ref, grow_ref, kt_ref)


def _prologue_kernel(x_ref, g_ref, scale_ref, shift_ref, wg_ref, bg_ref, wk_ref, h_ref, gcol_ref, grow_ref, kt_ref):
    x = x_ref[...]
    y = x * lax.rsqrt(jnp.mean(x * x, axis=-1, keepdims=True) + EPS)
    h = (y * g_ref[...]) * (1.0 + scale_ref[...]) + shift_ref[...]
    hb = h.astype(BF16)
    h_ref[...] = hb
    kt_ref[...] = _dot_nt(wk_ref[...].astype(BF16), hb).astype(BF16)
    _scan_gate_terms(hb, wg_ref[...], bg_ref[...], gcol_ref, grow_ref)


def _split3(x):
    hi = x.astype(BF16)
    rest = x - hi.astype(F32)
    mid = rest.astype(BF16)
    return hi, mid, (rest - mid.astype(F32)).astype(BF16)


def _scan_gate_terms(hb, wg, bg_row, gcol_ref, grow_ref):
    H = M_HEADS
    tm = hb.shape[0]
    lane = lax.broadcasted_iota(jnp.int32, (tm, LANES), 1)
    pre = _dot_nt(hb, wg.astype(BF16)) + bg_row
    gates = jnp.where((lane & H) != 0, _log_sigmoid(pre), pre)
    t_idx = lax.broadcasted_iota(jnp.int32, (tm, tm), 0)
    s_idx = lax.broadcasted_iota(jnp.int32, (tm, tm), 1)
    same_chunk = (t_idx // CHUNK) == (s_idx // CHUNK)
    at_or_before = jnp.logical_and(same_chunk, s_idx <= t_idx)
    at_or_after = jnp.logical_and(same_chunk, s_idx >= t_idx)
    lower = jnp.where(at_or_before, 1.0, 0.0).astype(BF16)
    upper = jnp.where(at_or_after, 1.0, 0.0).astype(BF16)
    parts = _split3(gates)
    sum_before = sum(_dot(lower, p) for p in parts)
    sum_after = sum(_dot(upper, p) for p in parts)
    b_col = jnp.where(lane >= 2 * H, sum_after, sum_before)
    gates_row, b_row = gates.T, b_col.T
    a_row = jnp.concatenate([gates_row[0:H] - b_row[H:2 * H],
                             gates_row[2 * H:3 * H] - b_row[3 * H:4 * H]], axis=0)
    cm_cols = []
    for st in range(2 * H):
        seen = at_or_after if st >= H else at_or_before
        cm_cols.append(jnp.max(jnp.where(seen, a_row[st:st + 1, :], -jnp.inf), axis=1, keepdims=True))
    gcol_ref[...] = jnp.concatenate(cm_cols + [b_col[:, H:2 * H], b_col[:, 3 * H:4 * H]], axis=1)
    grow_ref[...] = a_row


def _prologue(tokens, mod_rows, g_norm, w_t, layer, gate_row, qw, b_gate, ctx_len):
    first = isinstance(tokens, tuple)
    if first:
        ctx, lat = tokens
        bsz, seq, d = lat.shape
        t_all = ctx_len + seq
    else:
        bsz, t_all, d = tokens.shape
    tm = _tile(math.gcd(ctx_len, t_all - ctx_len), 256, LANES)
    n_ctx_tiles = ctx_len // tm
    assert gate_row % LANES == 0 and gate_row + LANES <= w_t.shape[1]

    def mod_row(b, t):
        return jnp.where(t < n_ctx_tiles, bsz, b)

    tile_spec = pl.BlockSpec((None, tm, d), lambda b, t: (b, t, 0))
    if first:
        kern = functools.partial(_prologue_first_kernel, n_ctx_tiles=n_ctx_tiles)
        token_specs = [
            pl.BlockSpec((None, tm, d), lambda b, t: (b, jnp.minimum(t, n_ctx_tiles - 1), 0)),
            pl.BlockSpec((None, tm, d), lambda b, t: (b, jnp.maximum(t - n_ctx_tiles, 0), 0)),
        ]
        token_args = [ctx, lat]
    else:
        kern, token_specs, token_args = _prologue_kernel, [tile_spec], [tokens]
    out_specs = [tile_spec,
                 pl.BlockSpec((None, tm, N_GATES), lambda b, t: (b, t, 0)),
                 pl.BlockSpec((None, 2 * M_HEADS, tm), lambda b, t: (b, 0, t)),
                 pl.BlockSpec((None, qw, tm), lambda b, t: (b, 0, t))]
    out_shape = [jax.ShapeDtypeStruct((bsz, t_all, d), BF16),
                 jax.ShapeDtypeStruct((bsz, t_all, N_GATES), F32),
                 jax.ShapeDtypeStruct((bsz, 2 * M_HEADS, t_all), F32),
                 jax.ShapeDtypeStruct((bsz, qw, t_all), BF16)]
    if first:
        out_specs.append(tile_spec)
        out_shape.append(jax.ShapeDtypeStruct((bsz, t_all, d), F32))
    return pl.pallas_call(
        kern,
        grid=(bsz, t_all // tm),
        in_specs=token_specs + [
            pl.BlockSpec((1, d), lambda b, t: (0, 0)),
            pl.BlockSpec((None, None, 1, d), lambda b, t: (mod_row(b, t), 1, 0, 0)),
            pl.BlockSpec((None, None, 1, d), lambda b, t: (mod_row(b, t), 0, 0, 0)),
            pl.BlockSpec((None, LANES, d), lambda b, t: (layer, gate_row // LANES, 0), pipeline_mode=pl.Buffered(1)),
            pl.BlockSpec((1, LANES), lambda b, t: (0, 0)),
            pl.BlockSpec((None, qw, d), lambda b, t: (layer, 1, 0), pipeline_mode=pl.Buffered(1)),
        ],
        out_specs=out_specs,
        out_shape=out_shape,
        compiler_params=_params("arbitrary", "arbitrary"),
        name="prologue",
    )(*token_args, g_norm.reshape(1, d), mod_rows, mod_rows, w_t,
      jnp.pad(b_gate, (0, LANES - N_GATES)).reshape(1, LANES), w_t)


def _rope_tables(seq, ctx_len):
    axis_dim = HEAD_DIM // 2
    rows = seq // GRID_W
    row_ids = jnp.repeat(jnp.arange(rows), GRID_W).astype(F32)
    col_ids = jnp.tile(jnp.arange(GRID_W), rows).astype(F32)
    inv = ROPE_THETA ** (-jnp.arange(0, axis_dim, 2, dtype=F32) / axis_dim)
    ang_r, ang_c = row_ids[:, None] * inv, col_ids[:, None] * inv
    cos_t = jnp.concatenate([jnp.cos(ang_r)] * 2 + [jnp.cos(ang_c)] * 2, axis=-1)
    sin_t = jnp.concatenate([-jnp.sin(ang_r), jnp.sin(ang_r), -jnp.sin(ang_c), jnp.sin(ang_c)], axis=-1)
    cos_t = jnp.concatenate([jnp.ones((ctx_len, HEAD_DIM), F32), cos_t], axis=0)
    sin_t = jnp.concatenate([jnp.zeros((ctx_len, HEAD_DIM), F32), sin_t], axis=0)
    return cos_t, sin_t


def _skip_mlstm_k(j, mq_hi, mv_lo):
    return j + jnp.where(j >= mq_hi, mv_lo - mq_hi, 0)


def _head_norm_rope(x, g, cos_t, sin_t):
    quarter = HEAD_DIM // 4
    lane = lax.broadcasted_iota(jnp.int32, x.shape, 1)
    first_half = (lane % (2 * quarter)) < quarter
    xn = x * lax.rsqrt(jnp.mean(x * x, axis=-1, keepdims=True) + EPS) * g
    swapped = jnp.where(first_half, pltpu.roll(xn, HEAD_DIM - quarter, axis=1), pltpu.roll(xn, quarter, axis=1))
    return xn * cos_t + swapped * sin_t


def _in_proj_kernel(a_ref, w_ref, cos_ref, sin_ref, gq_ref, gk_ref, p_ref, wb_ref, *,
                    mq_hi, mq_scale, mv_lo, q_lo, kv_tile, k_heads):
    j = _skip_mlstm_k(pl.program_id(0), mq_hi, mv_lo)

    @pl.when(pl.program_id(1) == 0)
    def _():
        wb_ref[...] = w_ref[...].astype(BF16)

    def row_parts():
        part = a_ref.shape[0] // IN_PROJECTION_ROW_PARTS
        for p in range(IN_PROJECTION_ROW_PARTS):
            rows = slice(p * part, (p + 1) * part)
            yield rows, _dot_nt(a_ref[rows, :], wb_ref[...])

    def rotate_heads(rows, y, heads, gain, post_scale):
        for hd in heads:
            cols = slice(hd * HEAD_DIM, (hd + 1) * HEAD_DIM)
            rotated = _head_norm_rope(y[:, cols], gain, cos_ref[rows, :], sin_ref[rows, :])
            p_ref[rows, cols] = (rotated * post_scale).astype(BF16)

    n_heads = p_ref.shape[-1] // HEAD_DIM
    is_q = jnp.logical_and(j >= q_lo, j < kv_tile)

    @pl.when(jnp.logical_not(jnp.logical_or(is_q, j == kv_tile)))
    def _():
        scale = jnp.where(j < mq_hi, mq_scale, 1.0)
        for rows, y in row_parts():
            p_ref[rows, :] = (y * scale).astype(BF16)

    @pl.when(is_q)
    def _():
        for rows, y in row_parts():
            rotate_heads(rows, y, range(n_heads), gq_ref[...], HEAD_DIM ** -0.5 * LOG2_E)

    @pl.when(j == kv_tile)
    def _():
        for rows, y in row_parts():
            rotate_heads(rows, y, range(k_heads), gk_ref[...], 1.0)
            p_ref[rows, k_heads * HEAD_DIM:] = y[:, k_heads * HEAD_DIM:].astype(BF16)


def _in_projection(h3, w_t, layer, cos_t, sin_t, g_q, g_k, mw, aw):
    bsz, t_all, d = h3.shape
    m = bsz * t_all
    kvw = aw // GQA_GROUP
    qw = mw // 2
    tn = 2 * kvw
    assert qw % tn == 0 and aw % tn == 0
    tm = _tile(t_all, 576, 16 * IN_PROJECTION_ROW_PARTS)
    tiles_per_batch = t_all // tm
    mq_hi = qw // tn
    mv_lo = 2 * mq_hi
    q_lo = 4 * mw // tn
    kv_tile = q_lo + aw // tn
    n_j = kv_tile + 1 + aw // tn

    def tile_of(j):
        return _skip_mlstm_k(j, mq_hi, mv_lo)

    def w_row(j):
        return pl.multiple_of(j * tn + jnp.where(j >= q_lo, N_GATES, 0), N_GATES)

    def out_col(j):
        return jnp.where(j < mq_hi, j + (q_lo - mv_lo), jnp.where(j < q_lo, j - mv_lo, j - mq_hi))

    kern = functools.partial(_in_proj_kernel, mq_hi=mq_hi, mq_scale=(qw // M_HEADS) ** -0.5, mv_lo=mv_lo,
                             q_lo=q_lo, kv_tile=kv_tile, k_heads=kvw // HEAD_DIM)
    p = pl.pallas_call(
        kern,
        grid=(n_j - (mv_lo - mq_hi), m // tm),
        in_specs=[
            pl.BlockSpec((tm, d), lambda j, i: (i, 0)),
            pl.BlockSpec((None, pl.Element(tn), pl.Element(d)), lambda j, i: (layer, w_row(tile_of(j)), 0)),
            pl.BlockSpec((tm, HEAD_DIM), lambda j, i: (i % tiles_per_batch, 0)),
            pl.BlockSpec((tm, HEAD_DIM), lambda j, i: (i % tiles_per_batch, 0)),
            pl.BlockSpec((1, HEAD_DIM), lambda j, i: (0, 0)),
            pl.BlockSpec((1, HEAD_DIM), lambda j, i: (0, 0)),
        ],
        out_specs=pl.BlockSpec((tm, tn), lambda j, i: (i, out_col(tile_of(j)))),
        out_shape=jax.ShapeDtypeStruct((m, 3 * mw + qw + 2 * aw + 2 * kvw), BF16),
        scratch_shapes=[pltpu.VMEM((tn, d), BF16)],
        compiler_params=_params("arbitrary", "arbitrary"),
        name="in_projection",
    )(h3.reshape(m, d), w_t, cos_t, sin_t, g_q.reshape(1, HEAD_DIM), g_k.reshape(1, HEAD_DIM))
    return p.reshape(bsz, t_all, -1)


def _mlstm_chunk(qb, kt, vx, a_row, cm_col, b_col, cx, m, reverse):
    L = qb.shape[0]
    dv = vx.shape[1] - LANES
    t_idx = lax.broadcasted_iota(jnp.int32, (L, L), 0)
    s_idx = lax.broadcasted_iota(jnp.int32, (L, L), 1)
    seen = s_idx >= t_idx if reverse else s_idx <= t_idx
    last = 0 if reverse else L - 1
    m_run = jnp.maximum(m, cm_col)
    s = _dot(qb, kt) * jnp.exp(jnp.where(seen, a_row - m_run, -jnp.inf))
    nd = jnp.exp(m - m_run) * _dot(qb, cx.astype(BF16)) + _dot(s.astype(BF16), vx)
    h = nd[:, :dv] / jnp.maximum(jnp.abs(nd[:, dv:dv + 1]), jnp.exp(-(b_col + m_run)))
    m_last = m_run[last:last + 1, :]
    kw = (kt.astype(F32) * jnp.exp(a_row - m_last)).astype(BF16)
    cx_new = jnp.exp(m - m_last) * cx + _dot(kw, vx)
    return h, cx_new, b_col[last:last + 1, :] + m_last


def _mlstm_streams(q_ref, kt_ref, v_ref, gcol_ref, grow_ref, state_refs, reverse, emit):
    bsz = q_ref.shape[0]
    dk = q_ref.shape[-1] // M_HEADS
    dv = v_ref.shape[-1] // M_HEADS
    ones = jnp.ones((q_ref.shape[1], LANES), BF16)

    @pl.when(pl.program_id(0) == 0)
    def _():
        for ref in state_refs:
            ref[...] = jnp.zeros_like(ref)

    for b in range(bsz):
        for hd in range(M_HEADS):
            cx_ref, m_ref = state_refs[2 * (b * M_HEADS + hd):2 * (b * M_HEADS + hd) + 2]
            st = (M_HEADS if reverse else 0) + hd
            vx = jnp.concatenate([v_ref[b, :, hd * dv:(hd + 1) * dv], ones], axis=1)
            h, cx_new, m_new = _mlstm_chunk(
                q_ref[b, :, hd * dk:(hd + 1) * dk], kt_ref[b, hd * dk:(hd + 1) * dk, :], vx,
                grow_ref[b, st:st + 1, :], gcol_ref[b, :, st:st + 1],
                gcol_ref[b, :, 2 * M_HEADS + st:2 * M_HEADS + st + 1],
                cx_ref[...], m_ref[...], reverse)
            cx_ref[...] = cx_new
            m_ref[...] = m_new
            emit(b, hd, h)


def _mlstm_fwd_kernel(q_ref, kt_ref, v_ref, gcol_ref, grow_ref, h_ref, *state_refs):
    dv = v_ref.shape[-1] // M_HEADS

    def emit(b, hd, h):
        h_ref[b, :, hd * dv:(hd + 1) * dv] = h

    _mlstm_streams(q_ref, kt_ref, v_ref, gcol_ref, grow_ref, state_refs, False, emit)


def _mlstm_bwd_kernel(q_ref, kt_ref, v_ref, gcol_ref, grow_ref, hf_ref, o_ref, z_ref, gm_ref,
                      out_ref, *state_refs):
    dv = v_ref.shape[-1] // M_HEADS

    def emit(b, hd, h):
        cols = slice(hd * dv, (hd + 1) * dv)
        hs = hf_ref[b, :, cols] + h
        hn = hs * lax.rsqrt(jnp.mean(hs * hs, axis=-1, keepdims=True) + EPS) * gm_ref[:, cols]
        gated = hn * jax.nn.sigmoid(o_ref[b, :, cols].astype(F32)) * _silu(z_ref[b, :, cols].astype(F32))
        out_ref[b, :, cols] = gated.astype(out_ref.dtype)

    _mlstm_streams(q_ref, kt_ref, v_ref, gcol_ref, grow_ref, state_refs, True, emit)


def _mlstm(p, kt, gcol, grow, g_mlstm, ctx_len):
    bsz, t_all, _ = p.shape
    mw = g_mlstm.shape[-1]
    qw = mw // 2
    nc = t_all // CHUNK
    n_ctx = ctx_len // CHUNK
    dk, dv = qw // M_HEADS, mw // M_HEADS

    def rev_chunk(c):
        return jnp.where(c < n_ctx, n_ctx - 1 - c, nc - 1 - (c - n_ctx))

    def specs(chunk_of):
        return [
            pl.BlockSpec((bsz, CHUNK, qw), lambda c: (0, chunk_of(c), 3 * mw // qw)),
            pl.BlockSpec((bsz, qw, CHUNK), lambda c: (0, 0, chunk_of(c))),
            pl.BlockSpec((bsz, CHUNK, mw), lambda c: (0, chunk_of(c), 0)),
            pl.BlockSpec((bsz, CHUNK, N_GATES), lambda c: (0, chunk_of(c), 0)),
            pl.BlockSpec((bsz, 2 * M_HEADS, CHUNK), lambda c: (0, 0, chunk_of(c))),
        ]

    scratch = [pltpu.VMEM((dk, dv + LANES), F32), pltpu.VMEM((1, 1), F32)] * (bsz * M_HEADS)
    h_fwd = pl.pallas_call(
        _mlstm_fwd_kernel,
        grid=(nc,),
        in_specs=specs(lambda c: c),
        out_specs=pl.BlockSpec((bsz, CHUNK, mw), lambda c: (0, c, 0)),
        out_shape=jax.ShapeDtypeStruct((bsz, t_all, mw), F32),
        scratch_shapes=scratch,
        compiler_params=_params("arbitrary"),
        name="mlstm_fwd",
    )(p, kt, p, gcol, grow)
    return pl.pallas_call(
        _mlstm_bwd_kernel,
        grid=(nc,),
        in_specs=specs(rev_chunk) + [
            pl.BlockSpec((bsz, CHUNK, mw), lambda c: (0, rev_chunk(c), 0)),
            pl.BlockSpec((bsz, CHUNK, mw), lambda c: (0, rev_chunk(c), 1)),
            pl.BlockSpec((bsz, CHUNK, mw), lambda c: (0, rev_chunk(c), 2)),
            pl.BlockSpec((1, mw), lambda c: (0, 0)),
        ],
        out_specs=pl.BlockSpec((bsz, CHUNK, mw), lambda c: (0, rev_chunk(c), 0)),
        out_shape=jax.ShapeDtypeStruct((bsz, t_all, mw), BF16),
        scratch_shapes=scratch,
        compiler_params=_params("arbitrary"),
        name="mlstm_bwd",
    )(p, kt, p, gcol, grow, h_fwd, p, p, g_mlstm.reshape(1, mw))


def _attn_kernel(q_ref, k_ref, v_ref, z_ref, o_ref, vx_ref, *stage_refs, ctx_len, key_chunk, update_ctx):
    tq = q_ref.shape[0]
    t_all = k_ref.shape[0]
    n_split = len(stage_refs) // 2
    s_refs, p_refs = stage_refs[:n_split], stage_refs[n_split:]
    heads_per_split = GQA_GROUP // n_split
    rows_per_split = heads_per_split * tq

    @pl.when(pl.program_id(2) == 0)
    def _():
        vx_ref[:, :HEAD_DIM] = v_ref[...]
        vx_ref[:, HEAD_DIM:] = jnp.ones((t_all, HEAD_DIM), BF16)

    def attend(n_keys):
        n_tiles = n_keys // LANES

        def scores(sp):
            q = jnp.concatenate([q_ref[:, g * HEAD_DIM:(g + 1) * HEAD_DIM]
                                 for g in range(sp * heads_per_split, (sp + 1) * heads_per_split)], axis=0)
            for lo in range(0, n_keys, key_chunk):
                hi = min(lo + key_chunk, n_keys)
                s_refs[sp][:, lo:hi] = _dot_nt(q, k_ref[lo:hi, :])

        def softmax(sp):
            s_ref, p_ref = s_refs[sp], p_refs[sp]
            for rb in range(rows_per_split // SOFTMAX_ROWS):
                rows = slice(rb * SOFTMAX_ROWS, (rb + 1) * SOFTMAX_ROWS)
                m_lanes = s_ref[rows, 0:LANES]
                for t in range(1, n_tiles):
                    m_lanes = jnp.maximum(m_lanes, s_ref[rows, t * LANES:(t + 1) * LANES])
                m_rows = jnp.broadcast_to(jnp.max(m_lanes, axis=-1, keepdims=True), (SOFTMAX_ROWS, LANES))
                for t in range(n_tiles):
                    cols = slice(t * LANES, (t + 1) * LANES)
                    p_ref[rows, cols] = jnp.exp2(s_ref[rows, cols] - m_rows).astype(BF16)

        def values(sp):
            ov = _dot(p_refs[sp][:, 0:n_keys], vx_ref[0:n_keys, :])
            o = ov[:, :HEAD_DIM] / ov[:, HEAD_DIM:HEAD_DIM + 1]
            for gl in range(heads_per_split):
                g = sp * heads_per_split + gl
                cols = slice(g * HEAD_DIM, (g + 1) * HEAD_DIM)
                gate = _silu(z_ref[:, cols].astype(F32))
                o_ref[:, cols] = (o[gl * tq:(gl + 1) * tq] * gate).astype(o_ref.dtype)

        scores(0)
        for sp in range(n_split):
            softmax(sp)
            if sp + 1 < n_split:
                scores(sp + 1)
            values(sp)

    @pl.when(pl.program_id(2) == 0)
    def _():
        if update_ctx:
            attend(ctx_len)
        else:
            o_ref[...] = jnp.zeros_like(o_ref)

    @pl.when(pl.program_id(2) > 0)
    def _():
        attend(t_all)


def _attention(p, mw, aw, ctx_len, update_ctx):
    bsz, t_all, _ = p.shape
    kvw = aw // GQA_GROUP
    kv_heads = kvw // HEAD_DIM
    gw = GQA_GROUP * HEAD_DIM
    tq = ctx_len
    assert (t_all - ctx_len) % tq == 0 and tq % SOFTMAX_ROWS == 0
    base = 3 * mw + mw // 2
    q_blk = base // gw
    k_blk = (base + aw) // HEAD_DIM
    v_blk = (base + aw + kvw) // HEAD_DIM
    z_blk = (base + aw + 2 * kvw) // gw
    n_split = 2
    kern = functools.partial(_attn_kernel, ctx_len=ctx_len, key_chunk=512, update_ctx=update_ctx)
    return pl.pallas_call(
        kern,
        grid=(bsz, kv_heads, t_all // tq),
        in_specs=[
            pl.BlockSpec((None, tq, gw), lambda b, h, i: (b, i, q_blk + h)),
            pl.BlockSpec((None, t_all, HEAD_DIM), lambda b, h, i: (b, 0, k_blk + h)),
            pl.BlockSpec((None, t_all, HEAD_DIM), lambda b, h, i: (b, 0, v_blk + h)),
            pl.BlockSpec((None, tq, gw), lambda b, h, i: (b, i, z_blk + h)),
        ],
        out_specs=pl.BlockSpec((None, tq, gw), lambda b, h, i: (b, i, h)),
        out_shape=jax.ShapeDtypeStruct((bsz, t_all, aw), BF16),
        scratch_shapes=([pltpu.VMEM((t_all, 2 * HEAD_DIM), BF16)]
                        + [pltpu.VMEM((GQA_GROUP * tq // n_split, t_all), F32)] * n_split
                        + [pltpu.VMEM((GQA_GROUP * tq // n_split, t_all), BF16)] * n_split),
        compiler_params=_params("arbitrary", "arbitrary", "arbitrary"),
        name="attention",
    )(p, p, p, p)


def _out_proj_kernel(am_ref, aa_ref, wm_ref, wa_ref, x_ref, gl_ref, gc_ref, o_ref, wmb_ref, wab_ref, *,
                     ctx_len, tiles_per_batch):
    tm = x_ref.shape[0]

    @pl.when(pl.program_id(1) == 0)
    def _():
        wmb_ref[...] = wm_ref[...].astype(BF16)
        wab_ref[...] = wa_ref[...].astype(BF16)

    row0 = (pl.program_id(1) % tiles_per_batch) * tm
    part = tm // PROJECTION_ROW_PARTS
    for p in range(PROJECTION_ROW_PARTS):
        rows = slice(p * part, (p + 1) * part)
        y = _dot(am_ref[rows, :], wmb_ref[...]) + _dot(aa_ref[rows, :], wab_ref[...])
        row = row0 + p * part + lax.broadcasted_iota(jnp.int32, y.shape, 0)
        gate = jnp.where(row < ctx_len, gc_ref[...], gl_ref[...])
        o_ref[rows, :] = x_ref[rows, :] + gate * y


def _out_projection(m_out, a_out, w_out, layer, x_all, mod_rows, ctx_len):
    bsz, t_all, d = x_all.shape
    mw = m_out.shape[-1]
    aw = a_out.shape[-1]
    assert mw == aw
    tm = _tile(t_all, 1152, 16 * PROJECTION_ROW_PARTS)
    tpb = t_all // tm
    tn = _tile(d, 512, LANES)
    m = bsz * t_all
    kern = functools.partial(_out_proj_kernel, ctx_len=ctx_len, tiles_per_batch=tpb)
    out = pl.pallas_call(
        kern,
        grid=(d // tn, m // tm),
        in_specs=[
            pl.BlockSpec((tm, mw), lambda j, i: (i, 0)),
            pl.BlockSpec((tm, aw), lambda j, i: (i, 0)),
            pl.BlockSpec((None, mw, tn), lambda j, i: (layer, 0, j)),
            pl.BlockSpec((None, aw, tn), lambda j, i: (layer, 1, j)),
            pl.BlockSpec((tm, tn), lambda j, i: (i, j)),
            pl.BlockSpec((None, None, 1, tn), lambda j, i: (i // tpb, 2, 0, j)),
            pl.BlockSpec((None, None, 1, tn), lambda j, i: (bsz, 2, 0, j)),
        ],
        out_specs=pl.BlockSpec((tm, tn), lambda j, i: (i, j)),
        out_shape=jax.ShapeDtypeStruct((m, d), F32),
        scratch_shapes=[pltpu.VMEM((mw, tn), BF16), pltpu.VMEM((aw, tn), BF16)],
        compiler_params=_params("arbitrary", "arbitrary"),
        name="out_projection",
    )(m_out.reshape(m, mw), a_out.reshape(m, aw), w_out, w_out, x_all.reshape(m, d), mod_rows, mod_rows)
    return out.reshape(bsz, t_all, d)


def _final_norm_kernel(x_ref, g_ref, o_ref):
    x = x_ref[...]
    o_ref[...] = x * lax.rsqrt(jnp.mean(x * x, axis=-1, keepdims=True) + EPS) * g_ref[...]


def _final_norm(x_all, g_final, ctx_len):
    bsz, t_all, d = x_all.shape
    seq = t_all - ctx_len
    tm = _tile(math.gcd(ctx_len, seq), 256, 8)
    off = ctx_len // tm
    return pl.pallas_call(
        _final_norm_kernel,
        grid=(bsz, seq // tm),
        in_specs=[
            pl.BlockSpec((None, tm, d), lambda b, t: (b, t + off, 0)),
            pl.BlockSpec((1, d), lambda b, t: (0, 0)),
        ],
        out_specs=pl.BlockSpec((None, tm, d), lambda b, t: (b, t, 0)),
        out_shape=jax.ShapeDtypeStruct((bsz, seq, d), F32),
        compiler_params=_params("arbitrary", "arbitrary"),
        name="final_norm",
    )(x_all, g_final.reshape(1, d))


def kernel(x, c, ctx, c_ctx, w_mod, b_mod, g_norm, w_in, b_gate, g_mlstm, g_q, g_k, w_out, g_final):
    bsz, seq, d = x.shape
    ctx_len = ctx.shape[1]
    depth = w_mod.shape[0]
    mw = g_mlstm.shape[-1]
    aw = w_out.shape[1] - mw
    gate_col = 4 * mw
    assert mw == aw and w_in.shape[-1] == gate_col + N_GATES + aw * 5 // 2
    assert ctx_len % CHUNK == 0 and seq % CHUNK == 0 and seq % GRID_W == 0

    w_t = jnp.swapaxes(w_in, 1, 2)
    cos_t, sin_t = _rope_tables(seq, ctx_len)

    c_rows = jnp.concatenate([c, c_ctx[None, :]], axis=0)
    n_rows = bsz + 1
    c_rows = jnp.pad(c_rows, ((0, -n_rows % 8), (0, 0)))
    mod = _modulation(c_rows, w_mod, b_mod)
    mod = mod[:, :n_rows].reshape(depth, n_rows, 3, 1, d)

    x_all = None
    for layer in range(depth):
        update_ctx = layer < depth - 1
        prologue_args = (mod[layer], g_norm[layer], w_t, layer, gate_col, mw // 2, b_gate[layer], ctx_len)
        if layer == 0:
            h, gcol, grow, kt, x_all = _prologue((ctx, x), *prologue_args)
        else:
            h, gcol, grow, kt = _prologue(x_all, *prologue_args)
        p = _in_projection(h, w_t, layer, cos_t, sin_t, g_q[layer], g_k[layer], mw, aw)
        m_out = _mlstm(p, kt, gcol, grow, g_mlstm[layer], ctx_len)
        a_out = _attention(p, mw, aw, ctx_len, update_ctx)
        x_all = _out_projection(m_out, a_out, w_out, layer, x_all, mod[layer], ctx_len)
    return _final_norm(x_all, g_final, ctx_len)
```

```python
import functools
import math

import jax
import jax.numpy as jnp
from jax import lax
from jax.experimental import pallas as pl
from jax.experimental.pallas import tpu as pltpu

CHUNK = 256
M_HEADS = 4
HEAD_DIM = 128
GQA_GROUP = 4
GRID_W = 64
ROPE_THETA = 10000.0
EPS = 1e-6
N_GATES = 4 * M_HEADS
LOG2_E = 1.4426950408889634
SOFTMAX_ROWS = 128
PROJECTION_ROW_PARTS = 4
IN_PROJECTION_ROW_PARTS = 2

LANES = 128
V7X_VMEM_LIMIT_BYTES = 56 * 1024 * 1024

F32 = jnp.float32
BF16 = jnp.bfloat16


def _params(*sem):
    return pltpu.CompilerParams(dimension_semantics=sem, vmem_limit_bytes=V7X_VMEM_LIMIT_BYTES)


def _tile(total, target, multiple):
    best = None
    for t in range(multiple, min(total, target) + 1, multiple):
        if total % t == 0:
            best = t
    assert best is not None, (total, target, multiple)
    return best


def _dot(a, b):
    return jnp.dot(a, b, preferred_element_type=F32)


def _dot_nt(a, b):
    return lax.dot_general(a, b, (((1,), (1,)), ((), ())), preferred_element_type=F32)


def _dot_tn(a, b):
    return lax.dot_general(a, b, (((0,), (0,)), ((), ())), preferred_element_type=F32)


def _silu(x):
    return x * jax.nn.sigmoid(x)


def _mod_kernel(c_ref, w_ref, b_ref, o_ref):
    @pl.when(pl.program_id(1) == 0)
    def _():
        o_ref[...] = jnp.broadcast_to(b_ref[...], o_ref.shape)

    o_ref[...] += _dot(_silu(c_ref[...]).astype(BF16), w_ref[...].astype(BF16))


def _modulation(c_rows, w_mod, b_mod):
    depth, d, n = w_mod.shape
    rows = c_rows.shape[0]
    tk = _tile(d, 256, LANES)
    c_chunks = c_rows.reshape(rows, d // tk, tk).swapaxes(0, 1)
    return pl.pallas_call(
        _mod_kernel,
        grid=(depth, d // tk),
        in_specs=[
            pl.BlockSpec((None, rows, tk), lambda l, k: (k, 0, 0)),
            pl.BlockSpec((None, tk, n), lambda l, k: (l, k, 0)),
            pl.BlockSpec((None, 1, n), lambda l, k: (l, 0, 0)),
        ],
        out_specs=pl.BlockSpec((None, rows, n), lambda l, k: (l, 0, 0)),
        out_shape=jax.ShapeDtypeStruct((depth, rows, n), F32),
        compiler_params=_params("arbitrary", "arbitrary"),
        name="modulation",
    )(c_chunks, w_mod, b_mod.reshape(depth, 1, n))


def _log_sigmoid(x):
    return jnp.minimum(x, 0.0) - jnp.log1p(jnp.exp(-jnp.abs(x)))


def _prologue_first_kernel(ctx_ref, lat_ref, g_ref, scale_ref, shift_ref, wg_ref, bg_ref, wk_ref,
                           h_ref, gcol_ref, grow_ref, kt_ref, xall_ref, *, n_ctx_tiles):
    @pl.when(pl.program_id(1) < n_ctx_tiles)
    def _():
        xall_ref[...] = ctx_ref[...]

    @pl.when(pl.program_id(1) >= n_ctx_tiles)
    def _():
        xall_ref[...] = lat_ref[...]

    _prologue_kernel(xall_ref, g_ref, scale_ref, shift_ref, wg_ref, bg_ref, wk_ref, h_ref, gcol_ref, grow_ref, kt_ref)


def _prologue_kernel(x_ref, g_ref, scale_ref, shift_ref, wg_ref, bg_ref, wk_ref, h_ref, gcol_ref, grow_ref, kt_ref):
    x = x_ref[...]
    y = x * lax.rsqrt(jnp.mean(x * x, axis=-1, keepdims=True) + EPS)
    h = (y * g_ref[...]) * (1.0 + scale_ref[...]) + shift_ref[...]
    hb = h.astype(BF16)
    h_ref[...] = hb
    half = kt_ref.shape[0] // 2
    kt_ref[:half, :] = _dot_nt(wk_ref[:half, :].astype(BF16), hb).astype(BF16)
    _scan_gate_terms(hb, wg_ref[...], bg_ref[...], gcol_ref, grow_ref)
    kt_ref[half:, :] = _dot_nt(wk_ref[half:, :].astype(BF16), hb).astype(BF16)


def _split3(x):
    hi = x.astype(BF16)
    rest = x - hi.astype(F32)
    mid = rest.astype(BF16)
    return hi, mid, (rest - mid.astype(F32)).astype(BF16)


def _scan_gate_terms(hb, wg, bg_row, gcol_ref, grow_ref):
    H = M_HEADS
    tm = hb.shape[0]
    lane = lax.broadcasted_iota(jnp.int32, (tm, LANES), 1)
    pre = _dot_nt(hb, wg.astype(BF16)) + bg_row
    gates = jnp.where((lane & H) != 0, _log_sigmoid(pre), pre)
    t_idx = lax.broadcasted_iota(jnp.int32, (tm, tm), 0)
    s_idx = lax.broadcasted_iota(jnp.int32, (tm, tm), 1)
    same_chunk = (t_idx // CHUNK) == (s_idx // CHUNK)
    at_or_before = jnp.logical_and(same_chunk, s_idx <= t_idx)
    at_or_after = jnp.logical_and(same_chunk, s_idx >= t_idx)
    lower = jnp.where(at_or_before, 1.0, 0.0).astype(BF16)
    upper = jnp.where(at_or_after, 1.0, 0.0).astype(BF16)
    parts = _split3(gates)
    sum_before = sum(_dot(lower, p) for p in parts)
    sum_after = sum(_dot(upper, p) for p in parts)
    b_col = jnp.where(lane >= 2 * H, sum_after, sum_before)
    gates_row, b_row = gates.T, b_col.T
    a_row = jnp.concatenate([gates_row[0:H] - b_row[H:2 * H],
                             gates_row[2 * H:3 * H] - b_row[3 * H:4 * H]], axis=0)
    cm_cols = []
    for st in range(2 * H):
        seen = at_or_after if st >= H else at_or_before
        cm_cols.append(jnp.max(jnp.where(seen, a_row[st:st + 1, :], -jnp.inf), axis=1, keepdims=True))
    gcol_ref[...] = jnp.concatenate(cm_cols + [b_col[:, H:2 * H], b_col[:, 3 * H:4 * H]], axis=1)
    grow_ref[...] = a_row


def _prologue(tokens, mod_rows, g_norm, w_t, layer, gate_row, qw, b_gate, ctx_len):
    first = isinstance(tokens, tuple)
    if first:
        ctx, lat = tokens
        bsz, seq, d = lat.shape
        t_all = ctx_len + seq
    else:
        bsz, t_all, d = tokens.shape
    tm = _tile(math.gcd(ctx_len, t_all - ctx_len), 256, LANES)
    n_ctx_tiles = ctx_len // tm
    assert gate_row % LANES == 0 and gate_row + LANES <= w_t.shape[1]

    def mod_row(b, t):
        return jnp.where(t < n_ctx_tiles, bsz, b)

    tile_spec = pl.BlockSpec((None, tm, d), lambda b, t: (b, t, 0))
    if first:
        kern = functools.partial(_prologue_first_kernel, n_ctx_tiles=n_ctx_tiles)
        token_specs = [
            pl.BlockSpec((None, tm, d), lambda b, t: (b, jnp.minimum(t, n_ctx_tiles - 1), 0)),
            pl.BlockSpec((None, tm, d), lambda b, t: (b, jnp.maximum(t - n_ctx_tiles, 0), 0)),
        ]
        token_args = [ctx, lat]
    else:
        kern, token_specs, token_args = _prologue_kernel, [tile_spec], [tokens]
    out_specs = [tile_spec,
                 pl.BlockSpec((None, tm, N_GATES), lambda b, t: (b, t, 0)),
                 pl.BlockSpec((None, 2 * M_HEADS, tm), lambda b, t: (b, 0, t)),
                 pl.BlockSpec((None, qw, tm), lambda b, t: (b, 0, t))]
    out_shape = [jax.ShapeDtypeStruct((bsz, t_all, d), BF16),
                 jax.ShapeDtypeStruct((bsz, t_all, N_GATES), F32),
                 jax.ShapeDtypeStruct((bsz, 2 * M_HEADS, t_all), F32),
                 jax.ShapeDtypeStruct((bsz, qw, t_all), BF16)]
    if first:
        out_specs.append(tile_spec)
        out_shape.append(jax.ShapeDtypeStruct((bsz, t_all, d), F32))
    return pl.pallas_call(
        kern,
        grid=(bsz, t_all // tm),
        in_specs=token_specs + [
            pl.BlockSpec((1, d), lambda b, t: (0, 0)),
            pl.BlockSpec((None, None, 1, d), lambda b, t: (mod_row(b, t), 1, 0, 0)),
            pl.BlockSpec((None, None, 1, d), lambda b, t: (mod_row(b, t), 0, 0, 0)),
            pl.BlockSpec((None, LANES, d), lambda b, t: (layer, gate_row // LANES, 0), pipeline_mode=pl.Buffered(1)),
            pl.BlockSpec((1, LANES), lambda b, t: (0, 0)),
            pl.BlockSpec((None, qw, d), lambda b, t: (layer, 1, 0), pipeline_mode=pl.Buffered(1)),
        ],
        out_specs=out_specs,
        out_shape=out_shape,
        compiler_params=_params("arbitrary", "arbitrary"),
        name="prologue",
    )(*token_args, g_norm.reshape(1, d), mod_rows, mod_rows, w_t,
      jnp.pad(b_gate, (0, LANES - N_GATES)).reshape(1, LANES), w_t)


def _rope_tables(seq, ctx_len):
    axis_dim = HEAD_DIM // 2
    rows = seq // GRID_W
    row_ids = jnp.repeat(jnp.arange(rows), GRID_W).astype(F32)
    col_ids = jnp.tile(jnp.arange(GRID_W), rows).astype(F32)
    inv = ROPE_THETA ** (-jnp.arange(0, axis_dim, 2, dtype=F32) / axis_dim)
    ang_r, ang_c = row_ids[:, None] * inv, col_ids[:, None] * inv
    cos_t = jnp.concatenate([jnp.cos(ang_r)] * 2 + [jnp.cos(ang_c)] * 2, axis=-1)
    sin_t = jnp.concatenate([-jnp.sin(ang_r), jnp.sin(ang_r), -jnp.sin(ang_c), jnp.sin(ang_c)], axis=-1)
    cos_t = jnp.concatenate([jnp.ones((ctx_len, HEAD_DIM), F32), cos_t], axis=0)
    sin_t = jnp.concatenate([jnp.zeros((ctx_len, HEAD_DIM), F32), sin_t], axis=0)
    return cos_t, sin_t


def _skip_mlstm_k(j, mq_hi, mv_lo):
    return j + jnp.where(j >= mq_hi, mv_lo - mq_hi, 0)


def _head_norm_rope(x, g, cos_t, sin_t):
    quarter = HEAD_DIM // 4
    lane = lax.broadcasted_iota(jnp.int32, x.shape, 1)
    first_half = (lane % (2 * quarter)) < quarter
    xn = x * lax.rsqrt(jnp.mean(x * x, axis=-1, keepdims=True) + EPS) * g
    swapped = jnp.where(first_half, pltpu.roll(xn, HEAD_DIM - quarter, axis=1), pltpu.roll(xn, quarter, axis=1))
    return xn * cos_t + swapped * sin_t


def _in_proj_kernel(a_ref, w_ref, cos_ref, sin_ref, gq_ref, gk_ref, p_ref, wb_ref, *,
                    mq_hi, mq_scale, mv_lo, q_lo, kv_tile, k_heads):
    j = _skip_mlstm_k(pl.program_id(0), mq_hi, mv_lo)

    @pl.when(pl.program_id(1) == 0)
    def _():
        wb_ref[...] = w_ref[...].astype(BF16)

    def row_parts():
        part = a_ref.shape[0] // IN_PROJECTION_ROW_PARTS
        for p in range(IN_PROJECTION_ROW_PARTS):
            rows = slice(p * part, (p + 1) * part)
            yield rows, _dot_nt(a_ref[rows, :], wb_ref[...])

    def rotate_heads(rows, y, heads, gain, post_scale):
        for hd in heads:
            cols = slice(hd * HEAD_DIM, (hd + 1) * HEAD_DIM)
            rotated = _head_norm_rope(y[:, cols], gain, cos_ref[rows, :], sin_ref[rows, :])
            p_ref[rows, cols] = (rotated * post_scale).astype(BF16)

    n_heads = p_ref.shape[-1] // HEAD_DIM
    is_q = jnp.logical_and(j >= q_lo, j < kv_tile)

    @pl.when(jnp.logical_not(jnp.logical_or(is_q, j == kv_tile)))
    def _():
        scale = jnp.where(j < mq_hi, mq_scale, 1.0)
        for rows, y in row_parts():
            p_ref[rows, :] = (y * scale).astype(BF16)

    @pl.when(is_q)
    def _():
        for rows, y in row_parts():
            rotate_heads(rows, y, range(n_heads), gq_ref[...], HEAD_DIM ** -0.5 * LOG2_E)

    @pl.when(j == kv_tile)
    def _():
        for rows, y in row_parts():
            rotate_heads(rows, y, range(k_heads), gk_ref[...], 1.0)
            p_ref[rows, k_heads * HEAD_DIM:] = y[:, k_heads * HEAD_DIM:].astype(BF16)


def _in_projection(h3, w_t, layer, cos_t, sin_t, g_q, g_k, mw, aw):
    bsz, t_all, d = h3.shape
    m = bsz * t_all
    kvw = aw // GQA_GROUP
    qw = mw // 2
    tn = 2 * kvw
    assert qw % tn == 0 and aw % tn == 0
    tm = _tile(t_all, 576, 16 * IN_PROJECTION_ROW_PARTS)
    tiles_per_batch = t_all // tm
    mq_hi = qw // tn
    mv_lo = 2 * mq_hi
    q_lo = 4 * mw // tn
    kv_tile = q_lo + aw // tn
    n_j = kv_tile + 1 + aw // tn

    def tile_of(j):
        return _skip_mlstm_k(j, mq_hi, mv_lo)

    def w_row(j):
        return pl.multiple_of(j * tn + jnp.where(j >= q_lo, N_GATES, 0), N_GATES)

    def out_col(j):
        return jnp.where(j < mq_hi, j + (q_lo - mv_lo), jnp.where(j < q_lo, j - mv_lo, j - mq_hi))

    kern = functools.partial(_in_proj_kernel, mq_hi=mq_hi, mq_scale=(qw // M_HEADS) ** -0.5, mv_lo=mv_lo,
                             q_lo=q_lo, kv_tile=kv_tile, k_heads=kvw // HEAD_DIM)
    p = pl.pallas_call(
        kern,
        grid=(n_j - (mv_lo - mq_hi), m // tm),
        in_specs=[
            pl.BlockSpec((tm, d), lambda j, i: (i, 0)),
            pl.BlockSpec((None, pl.Element(tn), pl.Element(d)), lambda j, i: (layer, w_row(tile_of(j)), 0)),
            pl.BlockSpec((tm, HEAD_DIM), lambda j, i: (i % tiles_per_batch, 0)),
            pl.BlockSpec((tm, HEAD_DIM), lambda j, i: (i % tiles_per_batch, 0)),
            pl.BlockSpec((1, HEAD_DIM), lambda j, i: (0, 0)),
            pl.BlockSpec((1, HEAD_DIM), lambda j, i: (0, 0)),
        ],
        out_specs=pl.BlockSpec((tm, tn), lambda j, i: (i, out_col(tile_of(j)))),
        out_shape=jax.ShapeDtypeStruct((m, 3 * mw + qw + 2 * aw + 2 * kvw), BF16),
        scratch_shapes=[pltpu.VMEM((tn, d), BF16)],
        compiler_params=_params("arbitrary", "arbitrary"),
        name="in_projection",
    )(h3.reshape(m, d), w_t, cos_t, sin_t, g_q.reshape(1, HEAD_DIM), g_k.reshape(1, HEAD_DIM))
    return p.reshape(bsz, t_all, -1)


def _mlstm_chunk(qb, kt, vx, a_row, cm_col, b_col, cx, m, reverse):
    L = qb.shape[0]
    dv = vx.shape[1] - LANES
    t_idx = lax.broadcasted_iota(jnp.int32, (L, L), 0)
    s_idx = lax.broadcasted_iota(jnp.int32, (L, L), 1)
    seen = s_idx >= t_idx if reverse else s_idx <= t_idx
    last = 0 if reverse else L - 1
    m_run = jnp.maximum(m, cm_col)
    s = _dot(qb, kt) * jnp.exp(jnp.where(seen, a_row - m_run, -jnp.inf))
    nd = jnp.exp(m - m_run) * _dot(qb, cx.astype(BF16)) + _dot(s.astype(BF16), vx)
    h = nd[:, :dv] / jnp.maximum(jnp.abs(nd[:, dv:dv + 1]), jnp.exp(-(b_col + m_run)))
    m_last = m_run[last:last + 1, :]
    kw = (kt.astype(F32) * jnp.exp(a_row - m_last)).astype(BF16)
    cx_new = jnp.exp(m - m_last) * cx + _dot(kw, vx)
    return h, cx_new, b_col[last:last + 1, :] + m_last


def _mlstm_streams(q_ref, kt_ref, v_ref, gcol_ref, grow_ref, state_refs, reverse, emit):
    bsz = q_ref.shape[0]
    dk = q_ref.shape[-1] // M_HEADS
    dv = v_ref.shape[-1] // M_HEADS
    ones = jnp.ones((q_ref.shape[1], LANES), BF16)

    @pl.when(pl.program_id(0) == 0)
    def _():
        for ref in state_refs:
            ref[...] = jnp.zeros_like(ref)

    for b in range(bsz):
        for hd in range(M_HEADS):
            cx_ref, m_ref = state_refs[2 * (b * M_HEADS + hd):2 * (b * M_HEADS + hd) + 2]
            st = (M_HEADS if reverse else 0) + hd
            vx = jnp.concatenate([v_ref[b, :, hd * dv:(hd + 1) * dv], ones], axis=1)
            h, cx_new, m_new = _mlstm_chunk(
                q_ref[b, :, hd * dk:(hd + 1) * dk], kt_ref[b, hd * dk:(hd + 1) * dk, :], vx,
                grow_ref[b, st:st + 1, :], gcol_ref[b, :, st:st + 1],
                gcol_ref[b, :, 2 * M_HEADS + st:2 * M_HEADS + st + 1],
                cx_ref[...], m_ref[...], reverse)
            cx_ref[...] = cx_new
            m_ref[...] = m_new
            emit(b, hd, h)


def _mlstm_fwd_kernel(q_ref, kt_ref, v_ref, gcol_ref, grow_ref, h_ref, *state_refs):
    dv = v_ref.shape[-1] // M_HEADS

    def emit(b, hd, h):
        h_ref[b, :, hd * dv:(hd + 1) * dv] = h

    _mlstm_streams(q_ref, kt_ref, v_ref, gcol_ref, grow_ref, state_refs, False, emit)


def _mlstm_bwd_kernel(q_ref, kt_ref, v_ref, gcol_ref, grow_ref, hf_ref, o_ref, z_ref, gm_ref,
                      out_ref, *state_refs):
    dv = v_ref.shape[-1] // M_HEADS

    def emit(b, hd, h):
        cols = slice(hd * dv, (hd + 1) * dv)
        hs = hf_ref[b, :, cols] + h
        hn = hs * lax.rsqrt(jnp.mean(hs * hs, axis=-1, keepdims=True) + EPS) * gm_ref[:, cols]
        gated = hn * jax.nn.sigmoid(o_ref[b, :, cols].astype(F32)) * _silu(z_ref[b, :, cols].astype(F32))
        out_ref[b, :, cols] = gated.astype(out_ref.dtype)

    _mlstm_streams(q_ref, kt_ref, v_ref, gcol_ref, grow_ref, state_refs, True, emit)


def _mlstm(p, kt, gcol, grow, g_mlstm, ctx_len):
    bsz, t_all, _ = p.shape
    mw = g_mlstm.shape[-1]
    qw = mw // 2
    nc = t_all // CHUNK
    n_ctx = ctx_len // CHUNK
    dk, dv = qw // M_HEADS, mw // M_HEADS

    def rev_chunk(c):
        return jnp.where(c < n_ctx, n_ctx - 1 - c, nc - 1 - (c - n_ctx))

    def specs(chunk_of):
        return [
            pl.BlockSpec((bsz, CHUNK, qw), lambda c: (0, chunk_of(c), 3 * mw // qw)),
            pl.BlockSpec((bsz, qw, CHUNK), lambda c: (0, 0, chunk_of(c))),
            pl.BlockSpec((bsz, CHUNK, mw), lambda c: (0, chunk_of(c), 0)),
            pl.BlockSpec((bsz, CHUNK, N_GATES), lambda c: (0, chunk_of(c), 0)),
            pl.BlockSpec((bsz, 2 * M_HEADS, CHUNK), lambda c: (0, 0, chunk_of(c))),
        ]

    scratch = [pltpu.VMEM((dk, dv + LANES), F32), pltpu.VMEM((1, 1), F32)] * (bsz * M_HEADS)
    h_fwd = pl.pallas_call(
        _mlstm_fwd_kernel,
        grid=(nc,),
        in_specs=specs(lambda c: c),
        out_specs=pl.BlockSpec((bsz, CHUNK, mw), lambda c: (0, c, 0)),
        out_shape=jax.ShapeDtypeStruct((bsz, t_all, mw), F32),
        scratch_shapes=scratch,
        compiler_params=_params("arbitrary"),
        name="mlstm_fwd",
    )(p, kt, p, gcol, grow)
    return pl.pallas_call(
        _mlstm_bwd_kernel,
        grid=(nc,),
        in_specs=specs(rev_chunk) + [
            pl.BlockSpec((bsz, CHUNK, mw), lambda c: (0, rev_chunk(c), 0)),
            pl.BlockSpec((bsz, CHUNK, mw), lambda c: (0, rev_chunk(c), 1)),
            pl.BlockSpec((bsz, CHUNK, mw), lambda c: (0, rev_chunk(c), 2)),
            pl.BlockSpec((1, mw), lambda c: (0, 0)),
        ],
        out_specs=pl.BlockSpec((bsz, CHUNK, mw), lambda c: (0, rev_chunk(c), 0)),
        out_shape=jax.ShapeDtypeStruct((bsz, t_all, mw), BF16),
        scratch_shapes=scratch,
        compiler_params=_params("arbitrary"),
        name="mlstm_bwd",
    )(p, kt, p, gcol, grow, h_fwd, p, p, g_mlstm.reshape(1, mw))


def _attn_kernel(q_ref, k_ref, v_ref, z_ref, o_ref, vx_ref, *stage_refs, ctx_len, key_chunk, update_ctx):
    tq = q_ref.shape[0]
    t_all = k_ref.shape[0]
    n_split = len(stage_refs) // 2
    s_refs, p_refs = stage_refs[:n_split], stage_refs[n_split:]
    heads_per_split = GQA_GROUP // n_split
    rows_per_split = heads_per_split * tq

    @pl.when(pl.program_id(2) == 0)
    def _():
        vx_ref[:, :HEAD_DIM] = v_ref[...]
        vx_ref[:, HEAD_DIM:] = jnp.ones((t_all, HEAD_DIM), BF16)

    def attend(n_keys):
        n_tiles = n_keys // LANES

        def scores(sp):
            q = jnp.concatenate([q_ref[:, g * HEAD_DIM:(g + 1) * HEAD_DIM]
                                 for g in range(sp * heads_per_split, (sp + 1) * heads_per_split)], axis=0)
            for lo in range(0, n_keys, key_chunk):
                hi = min(lo + key_chunk, n_keys)
                s_refs[sp][:, lo:hi] = _dot_nt(q, k_ref[lo:hi, :])

        def softmax(sp):
            s_ref, p_ref = s_refs[sp], p_refs[sp]
            for rb in range(rows_per_split // SOFTMAX_ROWS):
                rows = slice(rb * SOFTMAX_ROWS, (rb + 1) * SOFTMAX_ROWS)
                m_lanes = s_ref[rows, 0:LANES]
                for t in range(1, n_tiles):
                    m_lanes = jnp.maximum(m_lanes, s_ref[rows, t * LANES:(t + 1) * LANES])
                m_rows = jnp.broadcast_to(jnp.max(m_lanes, axis=-1, keepdims=True), (SOFTMAX_ROWS, LANES))
                for t in range(n_tiles):
                    cols = slice(t * LANES, (t + 1) * LANES)
                    p_ref[rows, cols] = jnp.exp2(s_ref[rows, cols] - m_rows).astype(BF16)

        def values(sp):
            ov = _dot(p_refs[sp][:, 0:n_keys], vx_ref[0:n_keys, :])
            o = ov[:, :HEAD_DIM] / ov[:, HEAD_DIM:HEAD_DIM + 1]
            for gl in range(heads_per_split):
                g = sp * heads_per_split + gl
                cols = slice(g * HEAD_DIM, (g + 1) * HEAD_DIM)
                gate = _silu(z_ref[:, cols].astype(F32))
                o_ref[:, cols] = (o[gl * tq:(gl + 1) * tq] * gate).astype(o_ref.dtype)

        scores(0)
        for sp in range(n_split):
            softmax(sp)
            if sp + 1 < n_split:
                scores(sp + 1)
            values(sp)

    @pl.when(pl.program_id(2) == 0)
    def _():
        if update_ctx:
            attend(ctx_len)
        else:
            o_ref[...] = jnp.zeros_like(o_ref)

    @pl.when(pl.program_id(2) > 0)
    def _():
        attend(t_all)


def _attention(p, mw, aw, ctx_len, update_ctx):
    bsz, t_all, _ = p.shape
    kvw = aw // GQA_GROUP
    kv_heads = kvw // HEAD_DIM
    gw = GQA_GROUP * HEAD_DIM
    tq = ctx_len
    assert (t_all - ctx_len) % tq == 0 and tq % SOFTMAX_ROWS == 0
    base = 3 * mw + mw // 2
    q_blk = base // gw
    k_blk = (base + aw) // HEAD_DIM
    v_blk = (base + aw + kvw) // HEAD_DIM
    z_blk = (base + aw + 2 * kvw) // gw
    n_split = 2
    kern = functools.partial(_attn_kernel, ctx_len=ctx_len, key_chunk=512, update_ctx=update_ctx)
    return pl.pallas_call(
        kern,
        grid=(bsz, kv_heads, t_all // tq),
        in_specs=[
            pl.BlockSpec((None, tq, gw), lambda b, h, i: (b, i, q_blk + h)),
            pl.BlockSpec((None, t_all, HEAD_DIM), lambda b, h, i: (b, 0, k_blk + h)),
            pl.BlockSpec((None, t_all, HEAD_DIM), lambda b, h, i: (b, 0, v_blk + h)),
            pl.BlockSpec((None, tq, gw), lambda b, h, i: (b, i, z_blk + h)),
        ],
        out_specs=pl.BlockSpec((None, tq, gw), lambda b, h, i: (b, i, h)),
        out_shape=jax.ShapeDtypeStruct((bsz, t_all, aw), BF16),
        scratch_shapes=([pltpu.VMEM((t_all, 2 * HEAD_DIM), BF16)]
                        + [pltpu.VMEM((GQA_GROUP * tq // n_split, t_all), F32)] * n_split
                        + [pltpu.VMEM((GQA_GROUP * tq // n_split, t_all), BF16)] * n_split),
        compiler_params=_params("arbitrary", "arbitrary", "arbitrary"),
        name="attention",
    )(p, p, p, p)


def _out_proj_kernel(am_ref, aa_ref, wm_ref, wa_ref, x_ref, gl_ref, gc_ref, o_ref, wmb_ref, wab_ref, *,
                     ctx_len, tiles_per_batch):
    tm = x_ref.shape[0]

    @pl.when(pl.program_id(1) == 0)
    def _():
        wmb_ref[...] = wm_ref[...].astype(BF16)
        wab_ref[...] = wa_ref[...].astype(BF16)

    row0 = (pl.program_id(1) % tiles_per_batch) * tm
    part = tm // PROJECTION_ROW_PARTS
    for p in range(PROJECTION_ROW_PARTS):
        rows = slice(p * part, (p + 1) * part)
        y = _dot(am_ref[rows, :], wmb_ref[...]) + _dot(aa_ref[rows, :], wab_ref[...])
        row = row0 + p * part + lax.broadcasted_iota(jnp.int32, y.shape, 0)
        gate = jnp.where(row < ctx_len, gc_ref[...], gl_ref[...])
        o_ref[rows, :] = x_ref[rows, :] + gate * y


def _out_projection(m_out, a_out, w_out, layer, x_all, mod_rows, ctx_len):
    bsz, t_all, d = x_all.shape
    mw = m_out.shape[-1]
    aw = a_out.shape[-1]
    assert mw == aw
    tm = _tile(t_all, 1152, 16 * PROJECTION_ROW_PARTS)
    tpb = t_all // tm
    tn = _tile(d, 512, LANES)
    m = bsz * t_all
    kern = functools.partial(_out_proj_kernel, ctx_len=ctx_len, tiles_per_batch=tpb)
    out = pl.pallas_call(
        kern,
        grid=(d // tn, m // tm),
        in_specs=[
            pl.BlockSpec((tm, mw), lambda j, i: (i, 0)),
            pl.BlockSpec((tm, aw), lambda j, i: (i, 0)),
            pl.BlockSpec((None, mw, tn), lambda j, i: (layer, 0, j)),
            pl.BlockSpec((None, aw, tn), lambda j, i: (layer, 1, j)),
            pl.BlockSpec((tm, tn), lambda j, i: (i, j)),
            pl.BlockSpec((None, None, 1, tn), lambda j, i: (i // tpb, 2, 0, j)),
            pl.BlockSpec((None, None, 1, tn), lambda j, i: (bsz, 2, 0, j)),
        ],
        out_specs=pl.BlockSpec((tm, tn), lambda j, i: (i, j)),
        out_shape=jax.ShapeDtypeStruct((m, d), F32),
        scratch_shapes=[pltpu.VMEM((mw, tn), BF16), pltpu.VMEM((aw, tn), BF16)],
        compiler_params=_params("arbitrary", "arbitrary"),
        name="out_projection",
    )(m_out.reshape(m, mw), a_out.reshape(m, aw), w_out, w_out, x_all.reshape(m, d), mod_rows, mod_rows)
    return out.reshape(bsz, t_all, d)


def _final_norm_kernel(x_ref, g_ref, o_ref):
    x = x_ref[...]
    o_ref[...] = x * lax.rsqrt(jnp.mean(x * x, axis=-1, keepdims=True) + EPS) * g_ref[...]


def _final_norm(x_all, g_final, ctx_len):
    bsz, t_all, d = x_all.shape
    seq = t_all - ctx_len
    tm = _tile(math.gcd(ctx_len, seq), 256, 8)
    off = ctx_len // tm
    return pl.pallas_call(
        _final_norm_kernel,
        grid=(bsz, seq // tm),
        in_specs=[
            pl.BlockSpec((None, tm, d), lambda b, t: (b, t + off, 0)),
            pl.BlockSpec((1, d), lambda b, t: (0, 0)),
        ],
        out_specs=pl.BlockSpec((None, tm, d), lambda b, t: (b, t, 0)),
        out_shape=jax.ShapeDtypeStruct((bsz, seq, d), F32),
        compiler_params=_params("arbitrary", "arbitrary"),
        name="final_norm",
    )(x_all, g_final.reshape(1, d))


def kernel(x, c, ctx, c_ctx, w_mod, b_mod, g_norm, w_in, b_gate, g_mlstm, g_q, g_k, w_out, g_final):
    bsz, seq, d = x.shape
    ctx_len = ctx.shape[1]
    depth = w_mod.shape[0]
    mw = g_mlstm.shape[-1]
    aw = w_out.shape[1] - mw
    gate_col = 4 * mw
    assert mw == aw and w_in.shape[-1] == gate_col + N_GATES + aw * 5 // 2
    assert ctx_len % CHUNK == 0 and seq % CHUNK == 0 and seq % GRID_W == 0

    w_t = jnp.swapaxes(w_in, 1, 2)
    cos_t, sin_t = _rope_tables(seq, ctx_len)

    c_rows = jnp.concatenate([c, c_ctx[None, :]], axis=0)
    n_rows = bsz + 1
    c_rows = jnp.pad(c_rows, ((0, -n_rows % 8), (0, 0)))
    mod = _modulation(c_rows, w_mod, b_mod)
    mod = mod[:, :n_rows].reshape(depth, n_rows, 3, 1, d)

    x_all = None
    for layer in range(depth):
        update_ctx = layer < depth - 1
        prologue_args = (mod[layer], g_norm[layer], w_t, layer, gate_col, mw // 2, b_gate[layer], ctx_len)
        if layer == 0:
            h, gcol, grow, kt, x_all = _prologue((ctx, x), *prologue_args)
        else:
            h, gcol, grow, kt = _prologue(x_all, *prologue_args)
        p = _in_projection(h, w_t, layer, cos_t, sin_t, g_q[layer], g_k[layer], mw, aw)
        m_out = _mlstm(p, kt, gcol, grow, g_mlstm[layer], ctx_len)
        a_out = _attention(p, mw, aw, ctx_len, update_ctx)
        x_all = _out_projection(m_out, a_out, w_out, layer, x_all, mod[layer], ctx_len)
    return _final_norm(x_all, g_final, ctx_len)
```

```python
import functools
import math

import jax
import jax.numpy as jnp
from jax import lax
from jax.experimental import pallas as pl
from jax.experimental.pallas import tpu as pltpu

M_HEADS = 4
HEAD_DIM = 128
GQA_GROUP = 4
GRID_W = 64
ROPE_THETA = 10000.0
EPS = 1e-6
N_GATES = 4 * M_HEADS
LOG2_E = 1.4426950408889634

LANES = 128
BF16_ROWS = 16
V7X_VMEM_LIMIT_BYTES = 56 * 1024 * 1024

CHUNK = 256
TOKEN_TILE = 256
MODULATION_K_ROWS = 256
IN_PROJECTION_ROWS = 576
IN_PROJECTION_ROW_PARTS = 2
OUT_PROJECTION_ROWS = 1152
OUT_PROJECTION_COLS = 512
OUT_PROJECTION_ROW_PARTS = 4
ATTENTION_KEY_CHUNK = 512
ATTENTION_ROW_GROUPS = 2
SOFTMAX_ROWS = 128

F32 = jnp.float32
BF16 = jnp.bfloat16


def _params(*sem):
    return pltpu.CompilerParams(dimension_semantics=sem, vmem_limit_bytes=V7X_VMEM_LIMIT_BYTES)


def _tile(total, target, multiple):
    best = None
    for t in range(multiple, min(total, target) + 1, multiple):
        if total % t == 0:
            best = t
    assert best is not None, (total, target, multiple)
    return best


def _dot(a, b):
    return jnp.dot(a, b, preferred_element_type=F32)


def _dot_nt(a, b):
    return lax.dot_general(a, b, (((1,), (1,)), ((), ())), preferred_element_type=F32)


def _silu(x):
    return x * jax.nn.sigmoid(x)


def _mod_kernel(c_ref, w_ref, b_ref, o_ref):
    @pl.when(pl.program_id(1) == 0)
    def _():
        o_ref[...] = jnp.broadcast_to(b_ref[...], o_ref.shape)

    o_ref[...] += _dot(_silu(c_ref[...]).astype(BF16), w_ref[...].astype(BF16))


def _modulation(c_rows, w_mod, b_mod):
    depth, d, n = w_mod.shape
    rows = c_rows.shape[0]
    tk = _tile(d, MODULATION_K_ROWS, LANES)
    c_chunks = c_rows.reshape(rows, d // tk, tk).swapaxes(0, 1)
    return pl.pallas_call(
        _mod_kernel,
        grid=(depth, d // tk),
        in_specs=[
            pl.BlockSpec((None, rows, tk), lambda l, k: (k, 0, 0)),
            pl.BlockSpec((None, tk, n), lambda l, k: (l, k, 0)),
            pl.BlockSpec((None, 1, n), lambda l, k: (l, 0, 0)),
        ],
        out_specs=pl.BlockSpec((None, rows, n), lambda l, k: (l, 0, 0)),
        out_shape=jax.ShapeDtypeStruct((depth, rows, n), F32),
        compiler_params=_params("arbitrary", "arbitrary"),
        name="modulation",
    )(c_chunks, w_mod, b_mod.reshape(depth, 1, n))


def _log_sigmoid(x):
    return jnp.minimum(x, 0.0) - jnp.log1p(jnp.exp(-jnp.abs(x)))


def _prologue_first_kernel(ctx_ref, lat_ref, g_ref, scale_ref, shift_ref, wg_ref, bg_ref, wk_ref,
                           h_ref, gcol_ref, grow_ref, kt_ref, xall_ref, *, n_ctx_tiles):
    @pl.when(pl.program_id(1) < n_ctx_tiles)
    def _():
        xall_ref[...] = ctx_ref[...]

    @pl.when(pl.program_id(1) >= n_ctx_tiles)
    def _():
        xall_ref[...] = lat_ref[...]

    _prologue_kernel(xall_ref, g_ref, scale_ref, shift_ref, wg_ref, bg_ref, wk_ref, h_ref, gcol_ref, grow_ref, kt_ref)


def _prologue_kernel(x_ref, g_ref, scale_ref, shift_ref, wg_ref, bg_ref, wk_ref, h_ref, gcol_ref, grow_ref, kt_ref):
    x = x_ref[...]
    y = x * lax.rsqrt(jnp.mean(x * x, axis=-1, keepdims=True) + EPS)
    h = (y * g_ref[...]) * (1.0 + scale_ref[...]) + shift_ref[...]
    hb = h.astype(BF16)
    h_ref[...] = hb
    half = kt_ref.shape[0] // 2
    kt_ref[:half, :] = _dot_nt(wk_ref[:half, :].astype(BF16), hb).astype(BF16)
    _scan_gate_terms(hb, wg_ref[...], bg_ref[...], gcol_ref, grow_ref)
    kt_ref[half:, :] = _dot_nt(wk_ref[half:, :].astype(BF16), hb).astype(BF16)


def _split3(x):
    hi = x.astype(BF16)
    rest = x - hi.astype(F32)
    mid = rest.astype(BF16)
    return hi, mid, (rest - mid.astype(F32)).astype(BF16)


def _scan_gate_terms(hb, wg, bg_row, gcol_ref, grow_ref):
    H = M_HEADS
    tm = hb.shape[0]
    lane = lax.broadcasted_iota(jnp.int32, (tm, LANES), 1)
    pre = _dot_nt(hb, wg.astype(BF16)) + bg_row
    gates = jnp.where((lane & H) != 0, _log_sigmoid(pre), pre)
    t_idx = lax.broadcasted_iota(jnp.int32, (tm, tm), 0)
    s_idx = lax.broadcasted_iota(jnp.int32, (tm, tm), 1)
    same_chunk = (t_idx // CHUNK) == (s_idx // CHUNK)
    at_or_before = jnp.logical_and(same_chunk, s_idx <= t_idx)
    at_or_after = jnp.logical_and(same_chunk, s_idx >= t_idx)
    lower = jnp.where(at_or_before, 1.0, 0.0).astype(BF16)
    upper = jnp.where(at_or_after, 1.0, 0.0).astype(BF16)
    parts = _split3(gates)
    sum_before = sum(_dot(lower, p) for p in parts)
    sum_after = sum(_dot(upper, p) for p in parts)
    b_col = jnp.where(lane >= 2 * H, sum_after, sum_before)
    gates_row, b_row = gates.T, b_col.T
    a_row = jnp.concatenate([gates_row[0:H] - b_row[H:2 * H],
                             gates_row[2 * H:3 * H] - b_row[3 * H:4 * H]], axis=0)
    cm_cols = []
    for st in range(2 * H):
        seen = at_or_after if st >= H else at_or_before
        cm_cols.append(jnp.max(jnp.where(seen, a_row[st:st + 1, :], -jnp.inf), axis=1, keepdims=True))
    gcol_ref[...] = jnp.concatenate(cm_cols + [b_col[:, H:2 * H], b_col[:, 3 * H:4 * H]], axis=1)
    grow_ref[...] = a_row


def _prologue(tokens, mod_rows, g_norm, w_t, layer, gate_row, qw, b_gate, ctx_len):
    first = isinstance(tokens, tuple)
    if first:
        ctx, lat = tokens
        bsz, seq, d = lat.shape
        t_all = ctx_len + seq
    else:
        bsz, t_all, d = tokens.shape
    tm = _tile(math.gcd(ctx_len, t_all - ctx_len), TOKEN_TILE, LANES)
    n_ctx_tiles = ctx_len // tm
    assert gate_row % LANES == 0 and gate_row + LANES <= w_t.shape[1]

    def mod_row(b, t):
        return jnp.where(t < n_ctx_tiles, bsz, b)

    tile_spec = pl.BlockSpec((None, tm, d), lambda b, t: (b, t, 0))
    if first:
        kern = functools.partial(_prologue_first_kernel, n_ctx_tiles=n_ctx_tiles)
        token_specs = [
            pl.BlockSpec((None, tm, d), lambda b, t: (b, jnp.minimum(t, n_ctx_tiles - 1), 0)),
            pl.BlockSpec((None, tm, d), lambda b, t: (b, jnp.maximum(t - n_ctx_tiles, 0), 0)),
        ]
        token_args = [ctx, lat]
    else:
        kern, token_specs, token_args = _prologue_kernel, [tile_spec], [tokens]
    out_specs = [tile_spec,
                 pl.BlockSpec((None, tm, N_GATES), lambda b, t: (b, t, 0)),
                 pl.BlockSpec((None, 2 * M_HEADS, tm), lambda b, t: (b, 0, t)),
                 pl.BlockSpec((None, qw, tm), lambda b, t: (b, 0, t))]
    out_shape = [jax.ShapeDtypeStruct((bsz, t_all, d), BF16),
                 jax.ShapeDtypeStruct((bsz, t_all, N_GATES), F32),
                 jax.ShapeDtypeStruct((bsz, 2 * M_HEADS, t_all), F32),
                 jax.ShapeDtypeStruct((bsz, qw, t_all), BF16)]
    if first:
        out_specs.append(tile_spec)
        out_shape.append(jax.ShapeDtypeStruct((bsz, t_all, d), F32))
    return pl.pallas_call(
        kern,
        grid=(bsz, t_all // tm),
        in_specs=token_specs + [
            pl.BlockSpec((1, d), lambda b, t: (0, 0)),
            pl.BlockSpec((None, None, 1, d), lambda b, t: (mod_row(b, t), 1, 0, 0)),
            pl.BlockSpec((None, None, 1, d), lambda b, t: (mod_row(b, t), 0, 0, 0)),
            pl.BlockSpec((None, LANES, d), lambda b, t: (layer, gate_row // LANES, 0), pipeline_mode=pl.Buffered(1)),
            pl.BlockSpec((1, LANES), lambda b, t: (0, 0)),
            pl.BlockSpec((None, qw, d), lambda b, t: (layer, 1, 0), pipeline_mode=pl.Buffered(1)),
        ],
        out_specs=out_specs,
        out_shape=out_shape,
        compiler_params=_params("arbitrary", "arbitrary"),
        name="prologue",
    )(*token_args, g_norm.reshape(1, d), mod_rows, mod_rows, w_t,
      jnp.pad(b_gate, (0, LANES - N_GATES)).reshape(1, LANES), w_t)


def _rope_tables(seq, ctx_len):
    axis_dim = HEAD_DIM // 2
    rows = seq // GRID_W
    row_ids = jnp.repeat(jnp.arange(rows), GRID_W).astype(F32)
    col_ids = jnp.tile(jnp.arange(GRID_W), rows).astype(F32)
    inv = ROPE_THETA ** (-jnp.arange(0, axis_dim, 2, dtype=F32) / axis_dim)
    ang_r, ang_c = row_ids[:, None] * inv, col_ids[:, None] * inv
    cos_t = jnp.concatenate([jnp.cos(ang_r)] * 2 + [jnp.cos(ang_c)] * 2, axis=-1)
    sin_t = jnp.concatenate([-jnp.sin(ang_r), jnp.sin(ang_r), -jnp.sin(ang_c), jnp.sin(ang_c)], axis=-1)
    cos_t = jnp.concatenate([jnp.ones((ctx_len, HEAD_DIM), F32), cos_t], axis=0)
    sin_t = jnp.concatenate([jnp.zeros((ctx_len, HEAD_DIM), F32), sin_t], axis=0)
    return cos_t, sin_t


def _skip_mlstm_k(j, mq_hi, mv_lo):
    return j + jnp.where(j >= mq_hi, mv_lo - mq_hi, 0)


def _head_norm_rope(x, g, cos_t, sin_t):
    quarter = HEAD_DIM // 4
    lane = lax.broadcasted_iota(jnp.int32, x.shape, 1)
    first_half = (lane % (2 * quarter)) < quarter
    xn = x * lax.rsqrt(jnp.mean(x * x, axis=-1, keepdims=True) + EPS) * g
    swapped = jnp.where(first_half, pltpu.roll(xn, HEAD_DIM - quarter, axis=1), pltpu.roll(xn, quarter, axis=1))
    return xn * cos_t + swapped * sin_t


def _in_proj_kernel(a_ref, w_ref, cos_ref, sin_ref, gq_ref, gk_ref, p_ref, wb_ref, *,
                    mq_hi, mq_scale, mv_lo, q_lo, kv_tile, k_heads):
    j = _skip_mlstm_k(pl.program_id(0), mq_hi, mv_lo)

    @pl.when(pl.program_id(1) == 0)
    def _():
        wb_ref[...] = w_ref[...].astype(BF16)

    def row_parts():
        part = a_ref.shape[0] // IN_PROJECTION_ROW_PARTS
        for p in range(IN_PROJECTION_ROW_PARTS):
            rows = slice(p * part, (p + 1) * part)
            yield rows, _dot_nt(a_ref[rows, :], wb_ref[...])

    def rotate_heads(rows, y, heads, gain, post_scale):
        for hd in heads:
            cols = slice(hd * HEAD_DIM, (hd + 1) * HEAD_DIM)
            rotated = _head_norm_rope(y[:, cols], gain, cos_ref[rows, :], sin_ref[rows, :])
            p_ref[rows, cols] = (rotated * post_scale).astype(BF16)

    n_heads = p_ref.shape[-1] // HEAD_DIM
    is_q = jnp.logical_and(j >= q_lo, j < kv_tile)

    @pl.when(jnp.logical_not(jnp.logical_or(is_q, j == kv_tile)))
    def _():
        scale = jnp.where(j < mq_hi, mq_scale, 1.0)
        for rows, y in row_parts():
            p_ref[rows, :] = (y * scale).astype(BF16)

    @pl.when(is_q)
    def _():
        for rows, y in row_parts():
            rotate_heads(rows, y, range(n_heads), gq_ref[...], HEAD_DIM ** -0.5 * LOG2_E)

    @pl.when(j == kv_tile)
    def _():
        for rows, y in row_parts():
            rotate_heads(rows, y, range(k_heads), gk_ref[...], 1.0)
            p_ref[rows, k_heads * HEAD_DIM:] = y[:, k_heads * HEAD_DIM:].astype(BF16)


def _in_projection(h3, w_t, layer, cos_t, sin_t, g_q, g_k, mw, aw):
    bsz, t_all, d = h3.shape
    m = bsz * t_all
    kvw = aw // GQA_GROUP
    qw = mw // 2
    tn = 2 * kvw
    assert qw % tn == 0 and aw % tn == 0
    tm = _tile(t_all, IN_PROJECTION_ROWS, BF16_ROWS * IN_PROJECTION_ROW_PARTS)
    tiles_per_batch = t_all // tm
    mq_hi = qw // tn
    mv_lo = 2 * mq_hi
    q_lo = 4 * mw // tn
    kv_tile = q_lo + aw // tn
    n_j = kv_tile + 1 + aw // tn

    def tile_of(j):
        return _skip_mlstm_k(j, mq_hi, mv_lo)

    def w_row(j):
        return pl.multiple_of(j * tn + jnp.where(j >= q_lo, N_GATES, 0), N_GATES)

    def out_col(j):
        return jnp.where(j < mq_hi, j + (q_lo - mv_lo), jnp.where(j < q_lo, j - mv_lo, j - mq_hi))

    kern = functools.partial(_in_proj_kernel, mq_hi=mq_hi, mq_scale=(qw // M_HEADS) ** -0.5, mv_lo=mv_lo,
                             q_lo=q_lo, kv_tile=kv_tile, k_heads=kvw // HEAD_DIM)
    p = pl.pallas_call(
        kern,
        grid=(n_j - (mv_lo - mq_hi), m // tm),
        in_specs=[
            pl.BlockSpec((tm, d), lambda j, i: (i, 0)),
            pl.BlockSpec((None, pl.Element(tn), pl.Element(d)), lambda j, i: (layer, w_row(tile_of(j)), 0)),
            pl.BlockSpec((tm, HEAD_DIM), lambda j, i: (i % tiles_per_batch, 0)),
            pl.BlockSpec((tm, HEAD_DIM), lambda j, i: (i % tiles_per_batch, 0)),
            pl.BlockSpec((1, HEAD_DIM), lambda j, i: (0, 0)),
            pl.BlockSpec((1, HEAD_DIM), lambda j, i: (0, 0)),
        ],
        out_specs=pl.BlockSpec((tm, tn), lambda j, i: (i, out_col(tile_of(j)))),
        out_shape=jax.ShapeDtypeStruct((m, 3 * mw + qw + 2 * aw + 2 * kvw), BF16),
        scratch_shapes=[pltpu.VMEM((tn, d), BF16)],
        compiler_params=_params("arbitrary", "arbitrary"),
        name="in_projection",
    )(h3.reshape(m, d), w_t, cos_t, sin_t, g_q.reshape(1, HEAD_DIM), g_k.reshape(1, HEAD_DIM))
    return p.reshape(bsz, t_all, -1)


def _mlstm_chunk(qb, kt, vx, a_row, cm_col, b_col, cx, m, reverse):
    L = qb.shape[0]
    dv = vx.shape[1] - LANES
    t_idx = lax.broadcasted_iota(jnp.int32, (L, L), 0)
    s_idx = lax.broadcasted_iota(jnp.int32, (L, L), 1)
    seen = s_idx >= t_idx if reverse else s_idx <= t_idx
    last = 0 if reverse else L - 1
    m_run = jnp.maximum(m, cm_col)
    s = _dot(qb, kt) * jnp.exp(jnp.where(seen, a_row - m_run, -jnp.inf))
    nd = jnp.exp(m - m_run) * _dot(qb, cx.astype(BF16)) + _dot(s.astype(BF16), vx)
    h = nd[:, :dv] / jnp.maximum(jnp.abs(nd[:, dv:dv + 1]), jnp.exp(-(b_col + m_run)))
    m_last = m_run[last:last + 1, :]
    kw = (kt.astype(F32) * jnp.exp(a_row - m_last)).astype(BF16)
    cx_new = jnp.exp(m - m_last) * cx + _dot(kw, vx)
    return h, cx_new, b_col[last:last + 1, :] + m_last


def _mlstm_streams(q_ref, kt_ref, v_ref, gcol_ref, grow_ref, state_refs, reverse, emit):
    bsz = q_ref.shape[0]
    dk = q_ref.shape[-1] // M_HEADS
    dv = v_ref.shape[-1] // M_HEADS
    ones = jnp.ones((q_ref.shape[1], LANES), BF16)

    @pl.when(pl.program_id(0) == 0)
    def _():
        for ref in state_refs:
            ref[...] = jnp.zeros_like(ref)

    for b in range(bsz):
        for hd in range(M_HEADS):
            cx_ref, m_ref = state_refs[2 * (b * M_HEADS + hd):2 * (b * M_HEADS + hd) + 2]
            st = (M_HEADS if reverse else 0) + hd
            vx = jnp.concatenate([v_ref[b, :, hd * dv:(hd + 1) * dv], ones], axis=1)
            h, cx_new, m_new = _mlstm_chunk(
                q_ref[b, :, hd * dk:(hd + 1) * dk], kt_ref[b, hd * dk:(hd + 1) * dk, :], vx,
                grow_ref[b, st:st + 1, :], gcol_ref[b, :, st:st + 1],
                gcol_ref[b, :, 2 * M_HEADS + st:2 * M_HEADS + st + 1],
                cx_ref[...], m_ref[...], reverse)
            cx_ref[...] = cx_new
            m_ref[...] = m_new
            emit(b, hd, h)


def _mlstm_fwd_kernel(q_ref, kt_ref, v_ref, gcol_ref, grow_ref, h_ref, *state_refs):
    dv = v_ref.shape[-1] // M_HEADS

    def emit(b, hd, h):
        h_ref[b, :, hd * dv:(hd + 1) * dv] = h

    _mlstm_streams(q_ref, kt_ref, v_ref, gcol_ref, grow_ref, state_refs, False, emit)


def _mlstm_bwd_kernel(q_ref, kt_ref, v_ref, gcol_ref, grow_ref, hf_ref, o_ref, z_ref, gm_ref,
                      out_ref, *state_refs):
    dv = v_ref.shape[-1] // M_HEADS

    def emit(b, hd, h):
        cols = slice(hd * dv, (hd + 1) * dv)
        hs = hf_ref[b, :, cols] + h
        hn = hs * lax.rsqrt(jnp.mean(hs * hs, axis=-1, keepdims=True) + EPS) * gm_ref[:, cols]
        gated = hn * jax.nn.sigmoid(o_ref[b, :, cols].astype(F32)) * _silu(z_ref[b, :, cols].astype(F32))
        out_ref[b, :, cols] = gated.astype(out_ref.dtype)

    _mlstm_streams(q_ref, kt_ref, v_ref, gcol_ref, grow_ref, state_refs, True, emit)


def _mlstm(p, kt, gcol, grow, g_mlstm, ctx_len):
    bsz, t_all, _ = p.shape
    mw = g_mlstm.shape[-1]
    qw = mw // 2
    nc = t_all // CHUNK
    n_ctx = ctx_len // CHUNK
    dk, dv = qw // M_HEADS, mw // M_HEADS

    def rev_chunk(c):
        return jnp.where(c < n_ctx, n_ctx - 1 - c, nc - 1 - (c - n_ctx))

    def specs(chunk_of):
        return [
            pl.BlockSpec((bsz, CHUNK, qw), lambda c: (0, chunk_of(c), 3 * mw // qw)),
            pl.BlockSpec((bsz, qw, CHUNK), lambda c: (0, 0, chunk_of(c))),
            pl.BlockSpec((bsz, CHUNK, mw), lambda c: (0, chunk_of(c), 0)),
            pl.BlockSpec((bsz, CHUNK, N_GATES), lambda c: (0, chunk_of(c), 0)),
            pl.BlockSpec((bsz, 2 * M_HEADS, CHUNK), lambda c: (0, 0, chunk_of(c))),
        ]

    scratch = [pltpu.VMEM((dk, dv + LANES), F32), pltpu.VMEM((1, 1), F32)] * (bsz * M_HEADS)
    h_fwd = pl.pallas_call(
        _mlstm_fwd_kernel,
        grid=(nc,),
        in_specs=specs(lambda c: c),
        out_specs=pl.BlockSpec((bsz, CHUNK, mw), lambda c: (0, c, 0)),
        out_shape=jax.ShapeDtypeStruct((bsz, t_all, mw), F32),
        scratch_shapes=scratch,
        compiler_params=_params("arbitrary"),
        name="mlstm_fwd",
    )(p, kt, p, gcol, grow)
    return pl.pallas_call(
        _mlstm_bwd_kernel,
        grid=(nc,),
        in_specs=specs(rev_chunk) + [
            pl.BlockSpec((bsz, CHUNK, mw), lambda c: (0, rev_chunk(c), 0)),
            pl.BlockSpec((bsz, CHUNK, mw), lambda c: (0, rev_chunk(c), 1)),
            pl.BlockSpec((bsz, CHUNK, mw), lambda c: (0, rev_chunk(c), 2)),
            pl.BlockSpec((1, mw), lambda c: (0, 0)),
        ],
        out_specs=pl.BlockSpec((bsz, CHUNK, mw), lambda c: (0, rev_chunk(c), 0)),
        out_shape=jax.ShapeDtypeStruct((bsz, t_all, mw), BF16),
        scratch_shapes=scratch,
        compiler_params=_params("arbitrary"),
        name="mlstm_bwd",
    )(p, kt, p, gcol, grow, h_fwd, p, p, g_mlstm.reshape(1, mw))


def _attn_kernel(q_ref, k_ref, v_ref, z_ref, o_ref, vx_ref, *stage_refs, ctx_len, key_chunk, update_ctx):
    tq = q_ref.shape[0]
    t_all = k_ref.shape[0]
    n_split = len(stage_refs) // 2
    s_refs, p_refs = stage_refs[:n_split], stage_refs[n_split:]
    heads_per_split = GQA_GROUP // n_split
    rows_per_split = heads_per_split * tq

    @pl.when(pl.program_id(2) == 0)
    def _():
        vx_ref[:, :HEAD_DIM] = v_ref[...]
        vx_ref[:, HEAD_DIM:] = jnp.ones((t_all, HEAD_DIM), BF16)

    def attend(n_keys):
        n_tiles = n_keys // LANES

        def scores(sp):
            q = jnp.concatenate([q_ref[:, g * HEAD_DIM:(g + 1) * HEAD_DIM]
                                 for g in range(sp * heads_per_split, (sp + 1) * heads_per_split)], axis=0)
            for lo in range(0, n_keys, key_chunk):
                hi = min(lo + key_chunk, n_keys)
                s_refs[sp][:, lo:hi] = _dot_nt(q, k_ref[lo:hi, :])

        def softmax(sp):
            s_ref, p_ref = s_refs[sp], p_refs[sp]
            for rb in range(rows_per_split // SOFTMAX_ROWS):
                rows = slice(rb * SOFTMAX_ROWS, (rb + 1) * SOFTMAX_ROWS)
                m_lanes = s_ref[rows, 0:LANES]
                for t in range(1, n_tiles):
                    m_lanes = jnp.maximum(m_lanes, s_ref[rows, t * LANES:(t + 1) * LANES])
                m_rows = jnp.broadcast_to(jnp.max(m_lanes, axis=-1, keepdims=True), (SOFTMAX_ROWS, LANES))
                for t in range(n_tiles):
                    cols = slice(t * LANES, (t + 1) * LANES)
                    p_ref[rows, cols] = jnp.exp2(s_ref[rows, cols] - m_rows).astype(BF16)

        def values(sp):
            ov = _dot(p_refs[sp][:, 0:n_keys], vx_ref[0:n_keys, :])
            o = ov[:, :HEAD_DIM] / ov[:, HEAD_DIM:HEAD_DIM + 1]
            for gl in range(heads_per_split):
                g = sp * heads_per_split + gl
                cols = slice(g * HEAD_DIM, (g + 1) * HEAD_DIM)
                gate = _silu(z_ref[:, cols].astype(F32))
                o_ref[:, cols] = (o[gl * tq:(gl + 1) * tq] * gate).astype(o_ref.dtype)

        scores(0)
        for sp in range(n_split):
            softmax(sp)
            if sp + 1 < n_split:
                scores(sp + 1)
            values(sp)

    @pl.when(pl.program_id(2) == 0)
    def _():
        if update_ctx:
            attend(ctx_len)
        else:
            o_ref[...] = jnp.zeros_like(o_ref)

    @pl.when(pl.program_id(2) > 0)
    def _():
        attend(t_all)


def _attention(p, mw, aw, ctx_len, update_ctx):
    bsz, t_all, _ = p.shape
    kvw = aw // GQA_GROUP
    kv_heads = kvw // HEAD_DIM
    gw = GQA_GROUP * HEAD_DIM
    tq = ctx_len
    assert (t_all - ctx_len) % tq == 0 and tq % SOFTMAX_ROWS == 0
    base = 3 * mw + mw // 2
    q_blk = base // gw
    k_blk = (base + aw) // HEAD_DIM
    v_blk = (base + aw + kvw) // HEAD_DIM
    z_blk = (base + aw + 2 * kvw) // gw
    n_split = ATTENTION_ROW_GROUPS
    kern = functools.partial(_attn_kernel, ctx_len=ctx_len, key_chunk=ATTENTION_KEY_CHUNK, update_ctx=update_ctx)
    return pl.pallas_call(
        kern,
        grid=(bsz, kv_heads, t_all // tq),
        in_specs=[
            pl.BlockSpec((None, tq, gw), lambda b, h, i: (b, i, q_blk + h)),
            pl.BlockSpec((None, t_all, HEAD_DIM), lambda b, h, i: (b, 0, k_blk + h)),
            pl.BlockSpec((None, t_all, HEAD_DIM), lambda b, h, i: (b, 0, v_blk + h)),
            pl.BlockSpec((None, tq, gw), lambda b, h, i: (b, i, z_blk + h)),
        ],
        out_specs=pl.BlockSpec((None, tq, gw), lambda b, h, i: (b, i, h)),
        out_shape=jax.ShapeDtypeStruct((bsz, t_all, aw), BF16),
        scratch_shapes=([pltpu.VMEM((t_all, 2 * HEAD_DIM), BF16)]
                        + [pltpu.VMEM((GQA_GROUP * tq // n_split, t_all), F32)] * n_split
                        + [pltpu.VMEM((GQA_GROUP * tq // n_split, t_all), BF16)] * n_split),
        compiler_params=_params("arbitrary", "arbitrary", "arbitrary"),
        name="attention",
    )(p, p, p, p)


def _out_proj_kernel(am_ref, aa_ref, wm_ref, wa_ref, x_ref, gl_ref, gc_ref, o_ref, wmb_ref, wab_ref, *,
                     ctx_len, tiles_per_batch):
    tm = x_ref.shape[0]

    @pl.when(pl.program_id(1) == 0)
    def _():
        wmb_ref[...] = wm_ref[...].astype(BF16)
        wab_ref[...] = wa_ref[...].astype(BF16)

    row0 = (pl.program_id(1) % tiles_per_batch) * tm
    part = tm // OUT_PROJECTION_ROW_PARTS
    for p in range(OUT_PROJECTION_ROW_PARTS):
        rows = slice(p * part, (p + 1) * part)
        y = _dot(am_ref[rows, :], wmb_ref[...]) + _dot(aa_ref[rows, :], wab_ref[...])
        row = row0 + p * part + lax.broadcasted_iota(jnp.int32, y.shape, 0)
        gate = jnp.where(row < ctx_len, gc_ref[...], gl_ref[...])
        o_ref[rows, :] = x_ref[rows, :] + gate * y


def _out_projection(m_out, a_out, w_out, layer, x_all, mod_rows, ctx_len):
    bsz, t_all, d = x_all.shape
    mw = m_out.shape[-1]
    aw = a_out.shape[-1]
    assert mw == aw
    tm = _tile(t_all, OUT_PROJECTION_ROWS, BF16_ROWS * OUT_PROJECTION_ROW_PARTS)
    tpb = t_all // tm
    tn = _tile(d, OUT_PROJECTION_COLS, LANES)
    m = bsz * t_all
    kern = functools.partial(_out_proj_kernel, ctx_len=ctx_len, tiles_per_batch=tpb)
    out = pl.pallas_call(
        kern,
        grid=(d // tn, m // tm),
        in_specs=[
            pl.BlockSpec((tm, mw), lambda j, i: (i, 0)),
            pl.BlockSpec((tm, aw), lambda j, i: (i, 0)),
            pl.BlockSpec((None, mw, tn), lambda j, i: (layer, 0, j)),
            pl.BlockSpec((None, aw, tn), lambda j, i: (layer, 1, j)),
            pl.BlockSpec((tm, tn), lambda j, i: (i, j)),
            pl.BlockSpec((None, None, 1, tn), lambda j, i: (i // tpb, 2, 0, j)),
            pl.BlockSpec((None, None, 1, tn), lambda j, i: (bsz, 2, 0, j)),
        ],
        out_specs=pl.BlockSpec((tm, tn), lambda j, i: (i, j)),
        out_shape=jax.ShapeDtypeStruct((m, d), F32),
        scratch_shapes=[pltpu.VMEM((mw, tn), BF16), pltpu.VMEM((aw, tn), BF16)],
        compiler_params=_params("arbitrary", "arbitrary"),
        name="out_projection",
    )(m_out.reshape(m, mw), a_out.reshape(m, aw), w_out, w_out, x_all.reshape(m, d), mod_rows, mod_rows)
    return out.reshape(bsz, t_all, d)


def _final_norm_kernel(x_ref, g_ref, o_ref):
    x = x_ref[...]
    o_ref[...] = x * lax.rsqrt(jnp.mean(x * x, axis=-1, keepdims=True) + EPS) * g_ref[...]


def _final_norm(x_all, g_final, ctx_len):
    bsz, t_all, d = x_all.shape
    seq = t_all - ctx_len
    tm = _tile(math.gcd(ctx_len, seq), TOKEN_TILE, 8)
    off = ctx_len // tm
    return pl.pallas_call(
        _final_norm_kernel,
        grid=(bsz, seq // tm),
        in_specs=[
            pl.BlockSpec((None, tm, d), lambda b, t: (b, t + off, 0)),
            pl.BlockSpec((1, d), lambda b, t: (0, 0)),
        ],
        out_specs=pl.BlockSpec((None, tm, d), lambda b, t: (b, t, 0)),
        out_shape=jax.ShapeDtypeStruct((bsz, seq, d), F32),
        compiler_params=_params("arbitrary", "arbitrary"),
        name="final_norm",
    )(x_all, g_final.reshape(1, d))


def kernel(x, c, ctx, c_ctx, w_mod, b_mod, g_norm, w_in, b_gate, g_mlstm, g_q, g_k, w_out, g_final):
    bsz, seq, d = x.shape
    ctx_len = ctx.shape[1]
    depth = w_mod.shape[0]
    mw = g_mlstm.shape[-1]
    aw = w_out.shape[1] - mw
    gate_col = 4 * mw
    assert mw == aw and w_in.shape[-1] == gate_col + N_GATES + aw * 5 // 2
    assert ctx_len % CHUNK == 0 and seq % CHUNK == 0 and seq % GRID_W == 0

    w_t = jnp.swapaxes(w_in, 1, 2)
    cos_t, sin_t = _rope_tables(seq, ctx_len)

    c_rows = jnp.concatenate([c, c_ctx[None, :]], axis=0)
    n_rows = bsz + 1
    c_rows = jnp.pad(c_rows, ((0, -n_rows % 8), (0, 0)))
    mod = _modulation(c_rows, w_mod, b_mod)
    mod = mod[:, :n_rows].reshape(depth, n_rows, 3, 1, d)

    x_all = None
    for layer in range(depth):
        update_ctx = layer < depth - 1
        prologue_args = (mod[layer], g_norm[layer], w_t, layer, gate_col, mw // 2, b_gate[layer], ctx_len)
        if layer == 0:
            h, gcol, grow, kt, x_all = _prologue((ctx, x), *prologue_args)
        else:
            h, gcol, grow, kt = _prologue(x_all, *prologue_args)
        p = _in_projection(h, w_t, layer, cos_t, sin_t, g_q[layer], g_k[layer], mw, aw)
        m_out = _mlstm(p, kt, gcol, grow, g_mlstm[layer], ctx_len)
        a_out = _attention(p, mw, aw, ctx_len, update_ctx)
        x_all = _out_projection(m_out, a_out, w_out, layer, x_all, mod[layer], ctx_len)
    return _final_norm(x_all, g_final, ctx_len)
```
